```python
import math
import jax, jax.numpy as jnp
from jax import lax
import numpy as np

D_MODEL = 2048
BATCH = 1
SEQ = 16384
DEPTH = 1

CHUNK = 64
Q_BLOCK = 128
EPS = 1e-6
ROPE_THETA = 10000.0

M_HEADS = 4
M_DV = D_MODEL // 2 // M_HEADS
M_DQK = M_DV // 2
M_CONV = 4
GATE_CAP = 15.0
M_WIDTH = M_HEADS * M_DV

A_HEADS = 8
A_DH = D_MODEL // 2 // A_HEADS // 2
A_DV = 2 * A_DH
A_WIDTH = A_HEADS * A_DV

D_FF = -(-8 * D_MODEL // (3 * 256)) * 256

N_MQK = M_HEADS * M_DQK
IN_SIZES = (N_MQK, N_MQK, M_WIDTH, M_WIDTH, M_HEADS, M_HEADS,
            2 * A_HEADS * A_DH, 2 * A_HEADS * A_DH, A_WIDTH)
N_IN = sum(IN_SIZES)

kernel_name = "hybrid_mlstm_diffattn_block"


def rmsnorm(x, g):
    x32 = x.astype(jnp.float32)
    y = x32 * lax.rsqrt(jnp.mean(x32 * x32, axis=-1, keepdims=True) + EPS)
    return (y * g.astype(jnp.float32)).astype(x.dtype)


def split_proj(p):
    idx = np.cumsum(np.array(IN_SIZES))[:-1].tolist()
    return jnp.split(p, idx, axis=-1)


def rope(t, pos):
    half = t.shape[-1] // 2
    inv = ROPE_THETA ** (-jnp.arange(half, dtype=jnp.float32) / half)
    ang = pos.astype(jnp.float32)[:, None] * inv[None, :]
    cos = jnp.cos(ang).astype(t.dtype)
    sin = jnp.sin(ang).astype(t.dtype)
    t1, t2 = t[..., :half], t[..., half:]
    return jnp.concatenate([t1 * cos - t2 * sin, t2 * cos + t1 * sin], axis=-1)


def causal_conv(x, w, b):
    K = w.shape[0]
    S = x.shape[1]
    xp = jnp.pad(x, ((0, 0), (K - 1, 0), (0, 0)))
    y = b
    for j in range(K):
        y = y + xp[:, j:j + S] * w[j]
    return y


def mlstm_chunkwise(q, k, v, i_pre, f_pre):
    B, H, S, Dqk = q.shape
    Dv = v.shape[-1]
    NC = S // CHUNK
    q = q.astype(jnp.float32) * (Dqk ** -0.5)
    k = k.astype(jnp.float32)
    v = v.astype(jnp.float32)
    logf = jax.nn.log_sigmoid(f_pre.astype(jnp.float32))
    logi = i_pre.astype(jnp.float32)

    def to_chunks(t):
        return jnp.moveaxis(t.reshape((B, H, NC, CHUNK) + t.shape[3:]), 2, 0)

    qc, kc, vc, ic, fc = map(to_chunks, (q, k, v, logi, logf))
    causal = jnp.tril(jnp.ones((CHUNK, CHUNK), dtype=bool))

    def step(carry, xs):
        C, n, m = carry
        qb, kb, vb, ib, fb = xs
        b = jnp.cumsum(fb, axis=-1)
        logD = jnp.where(causal, b[..., :, None] - b[..., None, :] + ib[..., None, :], -jnp.inf)
        inter = b + m[..., None]
        m_t = jnp.maximum(inter, jnp.max(logD, axis=-1))
        Dw = jnp.exp(logD - m_t[..., None])
        inter_w = jnp.exp(inter - m_t)
        s = jnp.einsum('bhtd,bhsd->bhts', qb, kb) * Dw
        num = inter_w[..., None] * jnp.einsum('bhtd,bhde->bhte', qb, C) \
            + jnp.einsum('bhts,bhse->bhte', s, vb)
        den = inter_w * jnp.einsum('bhtd,bhd->bht', qb, n) + jnp.sum(s, axis=-1)
        h = num / jnp.maximum(jnp.abs(den), jnp.exp(-m_t))[..., None]
        bL = b[..., -1]
        w_log = bL[..., None] - b + ib
        m_new = jnp.maximum(bL + m, jnp.max(w_log, axis=-1))
        decay = jnp.exp(bL + m - m_new)
        kw = kb * jnp.exp(w_log - m_new[..., None])[..., None]
        C_new = decay[..., None, None] * C + jnp.einsum('bhsd,bhse->bhde', kw, vb)
        n_new = decay[..., None] * n + jnp.sum(kw, axis=2)
        return (C_new, n_new, m_new), h

    init = (jnp.zeros((B, H, Dqk, Dv), jnp.float32),
            jnp.zeros((B, H, Dqk), jnp.float32),
            jnp.zeros((B, H), jnp.float32))
    _, hc = lax.scan(step, init, (qc, kc, vc, ic, fc))
    return jnp.moveaxis(hc, 0, 2).reshape(B, H, S, Dv)


def diff_attention(q, k, v, lam):
    B, H, _, S, Dh = q.shape
    NB = S // Q_BLOCK
    scale = Dh ** -0.5
    key_chunk = jnp.arange(S) // CHUNK
    qb = jnp.moveaxis(q.reshape(B, H, 2, NB, Q_BLOCK, Dh), 3, 0)
    v32 = v.astype(jnp.float32)

    def block(args):
        qblk, start = args
        s = jnp.einsum('bhcqd,bhckd->bhcqk', qblk, k,
                       preferred_element_type=jnp.float32) * scale
        q_chunk = (start + jnp.arange(Q_BLOCK)) // CHUNK
        mask = key_chunk[None, :] <= q_chunk[:, None]
        p = jax.nn.softmax(jnp.where(mask, s, -jnp.inf), axis=-1)
        a = p[:, :, 0] - lam * p[:, :, 1]
        return jnp.einsum('bhqk,bhkv->bhqv', a, v32)

    out = lax.map(block, (qb, jnp.arange(NB) * Q_BLOCK))
    return jnp.moveaxis(out, 0, 2).reshape(B, H, S, v.shape[-1])


def setup_inputs(seed: int = 0) -> dict:
    key = jax.random.key(seed)
    ks = jax.random.split(key, 20)
    f32 = jnp.float32

    def nrm(k, shape, s):
        return s * jax.random.normal(k, shape, f32)

    return {
        "x": nrm(ks[0], (BATCH, SEQ, D_MODEL), 1.0),
        "norm1_g": 1.0 + nrm(ks[1], (DEPTH, D_MODEL), 0.02),
        "w_in": nrm(ks[2], (DEPTH, D_MODEL, N_IN), D_MODEL ** -0.5),
        "conv_w": nrm(ks[3], (DEPTH, M_CONV, 2 * N_MQK), M_CONV ** -0.5),
        "conv_b": nrm(ks[4], (DEPTH, 2 * N_MQK), 0.02),
        "b_igate": nrm(ks[5], (DEPTH, M_HEADS), 0.1),
        "b_fgate": jnp.linspace(3.0, 6.0, M_HEADS, dtype=f32)[None, :] + nrm(ks[6], (DEPTH, M_HEADS), 0.1),
        "mnorm_g": 1.0 + nrm(ks[7], (DEPTH, M_HEADS, M_DV), 0.02),
        "lambda_q1": nrm(ks[8], (DEPTH, A_DH), 0.1),
        "lambda_k1": nrm(ks[9], (DEPTH, A_DH), 0.1),
        "lambda_q2": nrm(ks[10], (DEPTH, A_DH), 0.1),
        "lambda_k2": nrm(ks[11], (DEPTH, A_DH), 0.1),
        "subln_g": 1.0 + nrm(ks[12], (DEPTH, A_DV), 0.02),
        "w_out": nrm(ks[13], (DEPTH, M_WIDTH + A_WIDTH, D_MODEL), (M_WIDTH + A_WIDTH) ** -0.5),
        "norm2_g": 1.0 + nrm(ks[14], (DEPTH, D_MODEL), 0.02),
        "w_gate": nrm(ks[15], (DEPTH, D_MODEL, D_FF), D_MODEL ** -0.5),
        "w_up": nrm(ks[16], (DEPTH, D_MODEL, D_FF), D_MODEL ** -0.5),
        "w_down": nrm(ks[17], (DEPTH, D_FF, D_MODEL), D_FF ** -0.5),
        "final_g": 1.0 + nrm(ks[18], (D_MODEL,), 0.02),
    }


def reference(x, norm1_g, w_in, conv_w, conv_b, b_igate, b_fgate, mnorm_g,
              lambda_q1, lambda_k1, lambda_q2, lambda_k2, subln_g, w_out,
              norm2_g, w_gate, w_up, w_down, final_g):
    B, S, _ = x.shape
    pos = jnp.arange(S)
    for l in range(DEPTH):
        h = rmsnorm(x, norm1_g[l])
        proj = h @ w_in[l]
        mq, mk, mv, mo, mi, mf, aq, ak, av = split_proj(proj)

        qk = jax.nn.silu(causal_conv(jnp.concatenate([mq, mk], axis=-1), conv_w[l], conv_b[l]))
        mq, mk = qk[..., :N_MQK], qk[..., N_MQK:]
        mq = mq.reshape(B, S, M_HEADS, M_DQK).transpose(0, 2, 1, 3)
        mk = mk.reshape(B, S, M_HEADS, M_DQK).transpose(0, 2, 1, 3)
        mv_h = mv.reshape(B, S, M_HEADS, M_DV).transpose(0, 2, 1, 3)
        i_pre = GATE_CAP * jnp.tanh((mi + b_igate[l]).astype(jnp.float32) / GATE_CAP)
        f_pre = GATE_CAP * jnp.tanh((mf + b_fgate[l]).astype(jnp.float32) / GATE_CAP)
        hm = mlstm_chunkwise(mq, mk, mv_h, i_pre.transpose(0, 2, 1), f_pre.transpose(0, 2, 1))
        hm = rmsnorm(hm, mnorm_g[l][:, None, :]).astype(x.dtype)
        hm = hm.transpose(0, 2, 1, 3).reshape(B, S, M_WIDTH) * jax.nn.sigmoid(mo)

        aq_h = rope(aq.reshape(B, S, A_HEADS, 2, A_DH).transpose(0, 2, 3, 1, 4), pos)
        ak_h = rope(ak.reshape(B, S, A_HEADS, 2, A_DH).transpose(0, 2, 3, 1, 4), pos)
        av_h = av.reshape(B, S, A_HEADS, A_DV).transpose(0, 2, 1, 3)
        lam_init = 0.8 - 0.6 * math.exp(-0.3 * l)
        lam = (jnp.exp(jnp.sum(lambda_q1[l].astype(jnp.float32) * lambda_k1[l].astype(jnp.float32)))
               - jnp.exp(jnp.sum(lambda_q2[l].astype(jnp.float32) * lambda_k2[l].astype(jnp.float32)))
               + lam_init)
        ha = diff_attention(aq_h, ak_h, av_h, lam)
        ha = rmsnorm(ha, subln_g[l]) * (1.0 - lam_init)
        ha = ha.transpose(0, 2, 1, 3).reshape(B, S, A_WIDTH).astype(x.dtype)

        x = x + jnp.concatenate([hm, ha], axis=-1) @ w_out[l]

        h2 = rmsnorm(x, norm2_g[l])
        x = x + (jax.nn.silu(h2 @ w_gate[l]) * (h2 @ w_up[l])) @ w_down[l]
    return rmsnorm(x, final_g)
```

```python
import functools
import math

import jax
import jax.numpy as jnp
from jax import lax
from jax.experimental import pallas as pl
from jax.experimental.pallas import tpu as pltpu

F32 = jnp.float32
BF16 = jnp.bfloat16

D_MODEL = 2048
SEQ = 16384
CHUNK = 64
EPS = 1e-6
ROPE_THETA = 10000.0

M_HEADS = 4
M_DV = 256
M_DQK = 128
M_CONV = 4
GATE_CAP = 15.0
M_WIDTH = M_HEADS * M_DV
N_MQK = M_HEADS * M_DQK

A_HEADS = 8
A_DH = 64
A_DV = 128
A_WIDTH = A_HEADS * A_DV
ROPE_HALF = A_DH // 2

D_FF = 5632

LANES = 128
VMEM_LIMIT = 56 * 1024 * 1024

NORM_TM = 512
PROJ_TM = 1024
PROJ_TN = 512
PROJT_TM = 512
ATT_BLK = 256
ML_L = 128
ML_R = 512
OUT_TM = 512
FF_TM = 512
FF_TF = 512

N_NAT = 2 * N_MQK + 2 * M_WIDTH + A_WIDTH
AK_OFF = 2 * N_MQK + 2 * M_WIDTH
ONES_W = LANES


def _cparams(sem):
    return pltpu.CompilerParams(dimension_semantics=sem, vmem_limit_bytes=VMEM_LIMIT)


def _norm_gates_kernel(x_ref, g_ref, wg_ref, bg_ref, h_ref, gc_ref, gr_ref):
    x = x_ref[...]
    ms = jnp.mean(x * x, axis=-1, keepdims=True)
    h = ((x * lax.rsqrt(ms + EPS)) * g_ref[...]).astype(BF16)
    h_ref[...] = h
    pre = jnp.dot(h, wg_ref[...], preferred_element_type=F32) + bg_ref[...]
    capped = GATE_CAP * jnp.tanh(pre / GATE_CAP)
    logf = -(jnp.maximum(-capped, 0.0) + jnp.log1p(jnp.exp(-jnp.abs(capped))))
    lane = lax.broadcasted_iota(jnp.int32, pre.shape, 1)
    gates = jnp.where(lane < M_HEADS, capped, logf)
    gc_ref[...] = gates[:, :2 * M_HEADS]
    gr_ref[...] = gates.T[:2 * M_HEADS, :]


def _norm_gates(x2d, g, wg, bg):
    S = x2d.shape[0]
    tm = NORM_TM
    return pl.pallas_call(
        _norm_gates_kernel,
        grid=(S // tm,),
        in_specs=[
            pl.BlockSpec((tm, D_MODEL), lambda i: (i, 0)),
            pl.BlockSpec((1, D_MODEL), lambda i: (0, 0)),
            pl.BlockSpec((D_MODEL, LANES), lambda i: (0, 0)),
            pl.BlockSpec((1, LANES), lambda i: (0, 0)),
        ],
        out_specs=[
            pl.BlockSpec((tm, D_MODEL), lambda i: (i, 0)),
            pl.BlockSpec((tm, 2 * M_HEADS), lambda i: (i, 0)),
            pl.BlockSpec((2 * M_HEADS, tm), lambda i: (0, i)),
        ],
        out_shape=[
            jax.ShapeDtypeStruct((S, D_MODEL), BF16),
            jax.ShapeDtypeStruct((S, 2 * M_HEADS), F32),
            jax.ShapeDtypeStruct((2 * M_HEADS, S), F32),
        ],
        compiler_params=_cparams(("parallel",)),
        name="norm_gates",
    )(x2d, g, wg, bg)


def _proj_nat_kernel(h_ref, w_ref, cos_ref, sin_ref, o_ref, *, rope_start):
    j = pl.program_id(1)
    r = jnp.dot(h_ref[...], w_ref[...], preferred_element_type=F32)

    @pl.when(j < rope_start)
    def _():
        o_ref[...] = r.astype(o_ref.dtype)

    @pl.when(j >= rope_start)
    def _():
        tn = r.shape[1]
        reps = tn // LANES
        c = jnp.concatenate([cos_ref[...]] * reps, axis=1)
        s = jnp.concatenate([sin_ref[...]] * reps, axis=1)
        lane = lax.broadcasted_iota(jnp.int32, r.shape, 1)
        first = (lane % A_DH) < ROPE_HALF
        swapped = jnp.where(first, pltpu.roll(r, tn - ROPE_HALF, 1), pltpu.roll(r, ROPE_HALF, 1))
        o_ref[...] = (r * c + swapped * s).astype(o_ref.dtype)


def _proj_nat(h, w_nat, cos_full, sin_signed):
    S = h.shape[0]
    tm, tn = PROJ_TM, PROJ_TN
    return pl.pallas_call(
        functools.partial(_proj_nat_kernel, rope_start=AK_OFF // tn),
        grid=(S // tm, N_NAT // tn),
        in_specs=[
            pl.BlockSpec((tm, D_MODEL), lambda i, j: (i, 0)),
            pl.BlockSpec((D_MODEL, tn), lambda i, j: (0, j)),
            pl.BlockSpec((tm, LANES), lambda i, j: (i, 0)),
            pl.BlockSpec((tm, LANES), lambda i, j: (i, 0)),
        ],
        out_specs=pl.BlockSpec((tm, tn), lambda i, j: (i, j)),
        out_shape=jax.ShapeDtypeStruct((S, N_NAT), BF16),
        compiler_params=_cparams(("parallel", "arbitrary")),
        name="proj_nat",
    )(h, w_nat, cos_full, sin_signed)


def _proj_t_kernel(h_ref, wt_ref, cos_ref, sin_ref, o_ref):
    h = h_ref[...]
    cT = cos_ref[...]
    sT = sin_ref[...]
    nblk = o_ref.shape[0]
    rows = 256
    scale = A_DH ** -0.5
    for rb in range(wt_ref.shape[0] // rows):
        r = lax.dot_general(wt_ref[rb * rows:(rb + 1) * rows, :], h,
                            (((1,), (1,)), ((), ())), preferred_element_type=F32)
        if rb * rows < A_WIDTH:
            parts = []
            for g in range(rows // A_DH):
                t1 = r[g * A_DH:g * A_DH + ROPE_HALF]
                t2 = r[g * A_DH + ROPE_HALF:(g + 1) * A_DH]
                parts.append((t1 * cT - t2 * sT) * scale)
                parts.append((t2 * cT + t1 * sT) * scale)
            r = jnp.concatenate(parts, axis=0)
        r = r.astype(o_ref.dtype)
        for b in range(nblk):
            o_ref[b, rb * rows:(rb + 1) * rows, :] = r[:, b * ATT_BLK:(b + 1) * ATT_BLK]


def _proj_t(h, wt, cosT, sinT):
    S = h.shape[0]
    tm = PROJT_TM
    nblk = tm // ATT_BLK
    nfeat = wt.shape[0]
    return pl.pallas_call(
        _proj_t_kernel,
        grid=(S // tm,),
        in_specs=[
            pl.BlockSpec((tm, D_MODEL), lambda i: (i, 0)),
            pl.BlockSpec((nfeat, D_MODEL), lambda i: (0, 0)),
            pl.BlockSpec((ROPE_HALF, tm), lambda i: (0, i)),
            pl.BlockSpec((ROPE_HALF, tm), lambda i: (0, i)),
        ],
        out_specs=pl.BlockSpec((nblk, nfeat, ATT_BLK), lambda i: (i, 0, 0)),
        out_shape=jax.ShapeDtypeStruct((S // ATT_BLK, nfeat, ATT_BLK), BF16),
        compiler_params=_cparams(("parallel",)),
        name="proj_t",
    )(h, wt, cosT, sinT)


def _split3(x):
    hi = x.astype(BF16)
    r1 = x - hi.astype(F32)
    mid = r1.astype(BF16)
    lo = (r1 - mid.astype(F32)).astype(BF16)
    return hi, mid, lo


def _mlstm_kernel(pm_ref, gc_ref, gr_ref, cw_ref, cb_ref, mg_ref, o_ref,
                  xbuf, c_ref, m_ref):
    R = pm_ref.shape[0]
    L = ML_L
    n_chunks = R // L
    NQK = 2 * N_MQK

    @pl.when(pl.program_id(0) == 0)
    def _():
        xbuf[0:8, :] = jnp.zeros((8, NQK), F32)
        c_ref[...] = jnp.zeros_like(c_ref)
        m_ref[...] = jnp.zeros_like(m_ref)

    x = pm_ref[:, 0:NQK].astype(F32)
    xbuf[8:8 + R, :] = x
    y = cb_ref[...]
    y = y + xbuf[5:5 + R, :] * cw_ref[0:1, :]
    y = y + xbuf[6:6 + R, :] * cw_ref[1:2, :]
    y = y + xbuf[7:7 + R, :] * cw_ref[2:3, :]
    y = y + x * cw_ref[3:4, :]
    xbuf[0:8, :] = x[R - 8:R, :]
    qk = y * jax.nn.sigmoid(y)
    q_all = (qk[:, :N_MQK] * (M_DQK ** -0.5)).astype(BF16)
    k_all = qk[:, N_MQK:]

    t_idx = lax.broadcasted_iota(jnp.int32, (L, L), 0)
    s_idx = lax.broadcasted_iota(jnp.int32, (L, L), 1)
    causal = s_idx <= t_idx
    tril = causal.astype(BF16)
    triu = (t_idx <= s_idx).astype(BF16)
    ones_ext = jnp.ones((L, ONES_W), BF16)

    for c in range(n_chunks):
        rows = slice(c * L, (c + 1) * L)
        gc = gc_ref[rows, :]
        gr = gr_ref[:, rows]
        b_col_all = sum(jnp.dot(tril, p, preferred_element_type=F32) for p in _split3(gc))
        b_row_all = sum(jnp.dot(p, triu, preferred_element_type=F32) for p in _split3(gr))
        for hd in range(M_HEADS):
            qh = q_all[rows, hd * M_DQK:(hd + 1) * M_DQK]
            kh = k_all[rows, hd * M_DQK:(hd + 1) * M_DQK]
            vh = pm_ref[rows, NQK + hd * M_DV:NQK + (hd + 1) * M_DV]
            v_ext = jnp.concatenate([vh, ones_ext], axis=1)
            i_row = gr[hd:hd + 1, :]
            b_row = b_row_all[M_HEADS + hd:M_HEADS + hd + 1, :]
            b_col = b_col_all[:, M_HEADS + hd:M_HEADS + hd + 1]

            logd = jnp.where(causal, b_col - b_row + i_row, -jnp.inf)
            m_loc = jnp.max(logd, axis=1, keepdims=True)
            d_loc = jnp.exp(logd - m_loc)
            s = lax.dot_general(qh, kh.astype(BF16), (((1,), (1,)), ((), ())),
                                preferred_element_type=F32)
            p_ext = jnp.dot((s * d_loc).astype(BF16), v_ext, preferred_element_type=F32)

            b_last = b_row[:, L - 1:L]
            w_log = b_last - b_row + i_row
            a_loc = jnp.max(w_log, axis=1, keepdims=True)
            kw_t = (kh.T * jnp.exp(w_log - a_loc)).astype(BF16)
            kv_ext = jnp.dot(kw_t, v_ext, preferred_element_type=F32)

            m_prev = m_ref[hd:hd + 1, 0:1]
            c_prev = c_ref[hd]
            inter = b_col + m_prev
            m_t = jnp.maximum(inter, m_loc)
            qc = jnp.dot(qh, c_prev.astype(BF16), preferred_element_type=F32)
            nd = jnp.exp(inter - m_t) * qc + jnp.exp(m_loc - m_t) * p_ext
            num = nd[:, :M_DV]
            den = nd[:, M_DV:M_DV + 1]
            hh = num / jnp.maximum(jnp.abs(den), jnp.exp(-m_t))

            m_new = jnp.maximum(b_last + m_prev, a_loc)
            c_ref[hd] = jnp.exp(b_last + m_prev - m_new) * c_prev + jnp.exp(a_loc - m_new) * kv_ext
            m_ref[hd:hd + 1, :] = jnp.broadcast_to(m_new, (1, m_ref.shape[1]))

            cols = slice(hd * M_DV, (hd + 1) * M_DV)
            ms = jnp.mean(hh * hh, axis=-1, keepdims=True)
            hn = (hh * lax.rsqrt(ms + EPS)) * mg_ref[:, cols]
            mo = pm_ref[rows, NQK + M_WIDTH + hd * M_DV:NQK + M_WIDTH + (hd + 1) * M_DV].astype(F32)
            o_ref[rows, cols] = (hn * jax.nn.sigmoid(mo)).astype(o_ref.dtype)


def _mlstm(pn, gc, gr, conv_w, conv_b, mnorm_g):
    S = pn.shape[0]
    R = ML_R
    wm = 2 * N_MQK + 2 * M_WIDTH
    return pl.pallas_call(
        _mlstm_kernel,
        grid=(S // R,),
        in_specs=[
            pl.BlockSpec((R, wm), lambda i: (i, 0)),
            pl.BlockSpec((R, 2 * M_HEADS), lambda i: (i, 0)),
            pl.BlockSpec((2 * M_HEADS, R), lambda i: (0, i)),
            pl.BlockSpec((M_CONV, 2 * N_MQK), lambda i: (0, 0)),
            pl.BlockSpec((1, 2 * N_MQK), lambda i: (0, 0)),
            pl.BlockSpec((1, M_WIDTH), lambda i: (0, 0)),
        ],
        out_specs=pl.BlockSpec((R, M_WIDTH), lambda i: (i, 0)),
        out_shape=jax.ShapeDtypeStruct((S, M_WIDTH), BF16),
        scratch_shapes=[
            pltpu.VMEM((R + 8, 2 * N_MQK), F32),
            pltpu.VMEM((M_HEADS, M_DQK, M_DV + ONES_W), F32),
            pltpu.VMEM((8, LANES), F32),
        ],
        compiler_params=_cparams(("arbitrary",)),
        name="mlstm",
    )(pn, gc, gr, conv_w, conv_b, mnorm_g)


def _attn_kernel(qT_ref, k_ref, vT_ref, lam_ref, g_ref, o_ref, *, lam_init):
    i = pl.program_id(1)
    blk = ATT_BLK
    qT = qT_ref[0]
    frow = lax.broadcasted_iota(jnp.int32, qT.shape, 0)
    zero = jnp.zeros_like(qT)
    qz = jnp.concatenate([jnp.where(frow < A_DH, qT, zero), jnp.where(frow >= A_DH, qT, zero)], axis=1)

    def step(j, carry, masked):
        m, l, acc = carry
        kj = k_ref[pl.ds(pl.multiple_of(j * blk, blk), blk), :]
        sT = jnp.dot(kj, qz, preferred_element_type=F32)
        if masked:
            kc = lax.broadcasted_iota(jnp.int32, sT.shape, 0) // CHUNK
            qc = (lax.broadcasted_iota(jnp.int32, sT.shape, 1) % blk) // CHUNK
            sT = jnp.where(kc <= qc, sT, -jnp.inf)
        m_new = jnp.maximum(m, jnp.max(sT, axis=0, keepdims=True))
        alpha = jnp.exp(m - m_new)
        p = jnp.exp(sT - m_new)
        l_new = alpha * l + jnp.sum(p, axis=0, keepdims=True)
        pv = jnp.dot(vT_ref[j], p.astype(BF16), preferred_element_type=F32)
        return m_new, l_new, alpha * acc + pv

    init = (jnp.full((1, 2 * blk), -jnp.inf, F32),
            jnp.zeros((1, 2 * blk), F32),
            jnp.zeros((A_DV, 2 * blk), F32))
    carry = lax.fori_loop(0, i, lambda j, c: step(j, c, False), init)
    m, l, acc = step(i, carry, True)

    lam_p = lam_ref[...]
    lam = (jnp.exp(jnp.sum(lam_p[0:1] * lam_p[1:2], axis=1, keepdims=True))
           - jnp.exp(jnp.sum(lam_p[2:3] * lam_p[3:4], axis=1, keepdims=True)) + lam_init)
    out = acc[:, :blk] / l[:, :blk] - lam * (acc[:, blk:] / l[:, blk:])
    ms = jnp.mean(out * out, axis=0, keepdims=True)
    y = ((out * lax.rsqrt(ms + EPS)) * g_ref[...]) * (1.0 - lam_init)
    o_ref[...] = y.T.astype(o_ref.dtype)


def _attn(qvT, pn, lam_params, subln_col, lam_init):
    nblk = qvT.shape[0]
    S = pn.shape[0]
    blk = ATT_BLK
    k_col0 = AK_OFF // (2 * A_DH)
    return pl.pallas_call(
        functools.partial(_attn_kernel, lam_init=lam_init),
        grid=(A_HEADS, nblk),
        in_specs=[
            pl.BlockSpec((1, 2 * A_DH, blk), lambda h, i: (i, h, 0)),
            pl.BlockSpec((S, 2 * A_DH), lambda h, i: (0, k_col0 + h)),
            pl.BlockSpec((nblk, A_DV, blk), lambda h, i: (0, A_HEADS + h, 0)),
            pl.BlockSpec((4, A_DH), lambda h, i: (0, 0)),
            pl.BlockSpec((A_DV, 1), lambda h, i: (0, 0)),
        ],
        out_specs=pl.BlockSpec((blk, A_DV), lambda h, i: (i, h)),
        out_shape=jax.ShapeDtypeStruct((S, A_WIDTH), BF16),
        compiler_params=_cparams(("parallel", "arbitrary")),
        name="diff_attn",
    )(qvT, pn, qvT, lam_params, subln_col)


def _out_proj_kernel(hm_ref, ha_ref, w_ref, x_ref, g_ref, x1_ref, h2_ref):
    r = jnp.dot(hm_ref[...], w_ref[0:M_WIDTH, :], preferred_element_type=F32)
    r = r + jnp.dot(ha_ref[...], w_ref[M_WIDTH:, :], preferred_element_type=F32)
    x1 = x_ref[...] + r
    x1_ref[...] = x1
    ms = jnp.mean(x1 * x1, axis=-1, keepdims=True)
    h2_ref[...] = ((x1 * lax.rsqrt(ms + EPS)) * g_ref[...]).astype(h2_ref.dtype)


def _out_proj(hm, ha, w_out, x2d, g):
    S = x2d.shape[0]
    tm = OUT_TM
    return pl.pallas_call(
        _out_proj_kernel,
        grid=(S // tm,),
        in_specs=[
            pl.BlockSpec((tm, M_WIDTH), lambda i: (i, 0)),
            pl.BlockSpec((tm, A_WIDTH), lambda i: (i, 0)),
            pl.BlockSpec((M_WIDTH + A_WIDTH, D_MODEL), lambda i: (0, 0)),
            pl.BlockSpec((tm, D_MODEL), lambda i: (i, 0)),
            pl.BlockSpec((1, D_MODEL), lambda i: (0, 0)),
        ],
        out_specs=[
            pl.BlockSpec((tm, D_MODEL), lambda i: (i, 0)),
            pl.BlockSpec((tm, D_MODEL), lambda i: (i, 0)),
        ],
        out_shape=[
            jax.ShapeDtypeStruct((S, D_MODEL), F32),
            jax.ShapeDtypeStruct((S, D_MODEL), BF16),
        ],
        compiler_params=_cparams(("parallel",)),
        name="out_proj",
    )(hm, ha, w_out, x2d, g)


def _swiglu_kernel(h2_ref, wg_ref, wu_ref, wd_ref, x1_ref, fg_ref, o_ref, acc_ref):
    j = pl.program_id(1)

    @pl.when(j == 0)
    def _():
        acc_ref[...] = jnp.zeros_like(acc_ref)

    h2 = h2_ref[...]
    g = jnp.dot(h2, wg_ref[...], preferred_element_type=F32)
    u = jnp.dot(h2, wu_ref[...], preferred_element_type=F32)
    a = ((g * jax.nn.sigmoid(g)) * u).astype(BF16)
    acc_ref[...] += jnp.dot(a, wd_ref[...], preferred_element_type=F32)

    @pl.when(j == pl.num_programs(1) - 1)
    def _():
        y = x1_ref[...] + acc_ref[...]
        ms = jnp.mean(y * y, axis=-1, keepdims=True)
        o_ref[...] = (y * lax.rsqrt(ms + EPS)) * fg_ref[...]


def _swiglu(h2, wg, wu, wd, x1, fg):
    S = h2.shape[0]
    tm, tf = FF_TM, FF_TF
    return pl.pallas_call(
        _swiglu_kernel,
        grid=(S // tm, D_FF // tf),
        in_specs=[
            pl.BlockSpec((tm, D_MODEL), lambda i, j: (i, 0)),
            pl.BlockSpec((D_MODEL, tf), lambda i, j: (0, j)),
            pl.BlockSpec((D_MODEL, tf), lambda i, j: (0, j)),
            pl.BlockSpec((tf, D_MODEL), lambda i, j: (j, 0)),
            pl.BlockSpec((tm, D_MODEL), lambda i, j: (i, 0)),
            pl.BlockSpec((1, D_MODEL), lambda i, j: (0, 0)),
        ],
        out_specs=pl.BlockSpec((tm, D_MODEL), lambda i, j: (i, 0)),
        out_shape=jax.ShapeDtypeStruct((S, D_MODEL), F32),
        scratch_shapes=[pltpu.VMEM((tm, D_MODEL), F32)],
        compiler_params=_cparams(("parallel", "arbitrary")),
        name="swiglu",
    )(h2, wg, wu, wd, x1, fg)


def _rope_tables(S):
    inv = ROPE_THETA ** (-jnp.arange(ROPE_HALF, dtype=F32) / ROPE_HALF)
    ang = jnp.arange(S).astype(F32)[:, None] * inv[None, :]
    cos = jnp.cos(ang)
    sin = jnp.sin(ang)
    cos_full = jnp.tile(cos, (1, LANES // ROPE_HALF))
    sin_signed = jnp.tile(jnp.concatenate([-sin, sin], axis=1), (1, LANES // A_DH))
    return cos_full, sin_signed, cos.T, sin.T


def kernel(x, norm1_g, w_in, conv_w, conv_b, b_igate, b_fgate, mnorm_g,
           lambda_q1, lambda_k1, lambda_q2, lambda_k2, subln_g, w_out,
           norm2_g, w_gate, w_up, w_down, final_g):
    B, S, D = x.shape
    assert (B, S, D) == (1, SEQ, D_MODEL) and norm1_g.shape[0] == 1
    l = 0
    x2d = x.reshape(S, D)

    w = w_in[l]
    o_mi = 2 * N_MQK + 2 * M_WIDTH
    o_aq = o_mi + 2 * M_HEADS
    o_ak = o_aq + A_WIDTH
    o_av = o_ak + A_WIDTH
    w_nat = jnp.concatenate([w[:, :o_mi], w[:, o_ak:o_av]], axis=1).astype(BF16)
    w_t = jnp.concatenate([w[:, o_aq:o_ak], w[:, o_av:]], axis=1).T.astype(BF16)
    w_g = jnp.pad(w[:, o_mi:o_aq], ((0, 0), (0, LANES - 2 * M_HEADS))).astype(BF16)
    b_g = jnp.pad(jnp.concatenate([b_igate[l], b_fgate[l]]), (0, LANES - 2 * M_HEADS)).reshape(1, LANES)

    cos_full, sin_signed, cosT, sinT = _rope_tables(S)

    h, gc, gr = _norm_gates(x2d, norm1_g[l].reshape(1, D), w_g, b_g)
    pn = _proj_nat(h, w_nat, cos_full, sin_signed)
    qvT = _proj_t(h, w_t, cosT, sinT)

    hm = _mlstm(pn, gc, gr, conv_w[l], conv_b[l].reshape(1, -1), mnorm_g[l].reshape(1, M_WIDTH))

    lam_init = 0.8 - 0.6 * math.exp(-0.3 * l)
    lam_params = jnp.stack([lambda_q1[l], lambda_k1[l], lambda_q2[l], lambda_k2[l]]).astype(F32)
    ha = _attn(qvT, pn, lam_params, subln_g[l].reshape(A_DV, 1), lam_init)

    x1, h2 = _out_proj(hm, ha, w_out[l].astype(BF16), x2d, norm2_g[l].reshape(1, D))
    out = _swiglu(h2, w_gate[l].astype(BF16), w_up[l].astype(BF16), w_down[l].astype(BF16),
                  x1, final_g.reshape(1, D))
    return out.reshape(B, S, D)
```

```python
import functools
import math

import jax
import jax.numpy as jnp
from jax import lax
from jax.experimental import pallas as pl
from jax.experimental.pallas import tpu as pltpu

F32 = jnp.float32
BF16 = jnp.bfloat16

D_MODEL = 2048
SEQ = 16384
CHUNK = 64
EPS = 1e-6
ROPE_THETA = 10000.0

M_HEADS = 4
M_DV = 256
M_DQK = 128
M_CONV = 4
GATE_CAP = 15.0
M_WIDTH = M_HEADS * M_DV
N_MQK = M_HEADS * M_DQK

A_HEADS = 8
A_DH = 64
A_DV = 128
A_WIDTH = A_HEADS * A_DV
ROPE_HALF = A_DH // 2

D_FF = 5632

LANES = 128
VMEM_LIMIT = 56 * 1024 * 1024

NORM_TM = 512
PROJ_TM = 1024
PROJ_TN = 512
PROJT_TM = 512
ATT_BLK = 512
ATT_TQ = 1024
ATT_TK = 512
LOG2E = math.log2(math.e)
ATT_L_LIMIT = 2.0 ** 60
ML_L = 128
ML_R = 512
OUT_TM = 512
FF_TM = 512
FF_TF = 512

N_NAT = 2 * N_MQK + 2 * M_WIDTH + A_WIDTH
AK_OFF = 2 * N_MQK + 2 * M_WIDTH
ONES_W = LANES


def _cparams(sem):
    return pltpu.CompilerParams(dimension_semantics=sem, vmem_limit_bytes=VMEM_LIMIT)


def _norm_gates_kernel(x_ref, g_ref, wg_ref, bg_ref, h_ref, gc_ref, gr_ref):
    x = x_ref[...]
    ms = jnp.mean(x * x, axis=-1, keepdims=True)
    h = ((x * lax.rsqrt(ms + EPS)) * g_ref[...]).astype(BF16)
    h_ref[...] = h
    pre = jnp.dot(h, wg_ref[...], preferred_element_type=F32) + bg_ref[...]
    capped = GATE_CAP * jnp.tanh(pre / GATE_CAP)
    logf = -(jnp.maximum(-capped, 0.0) + jnp.log1p(jnp.exp(-jnp.abs(capped))))
    lane = lax.broadcasted_iota(jnp.int32, pre.shape, 1)
    gates = jnp.where(lane < M_HEADS, capped, logf)
    gc_ref[...] = gates[:, :2 * M_HEADS]
    gr_ref[...] = gates.T[:2 * M_HEADS, :]


def _norm_gates(x2d, g, wg, bg):
    S = x2d.shape[0]
    tm = NORM_TM
    return pl.pallas_call(
        _norm_gates_kernel,
        grid=(S // tm,),
        in_specs=[
            pl.BlockSpec((tm, D_MODEL), lambda i: (i, 0)),
            pl.BlockSpec((1, D_MODEL), lambda i: (0, 0)),
            pl.BlockSpec((D_MODEL, LANES), lambda i: (0, 0)),
            pl.BlockSpec((1, LANES), lambda i: (0, 0)),
        ],
        out_specs=[
            pl.BlockSpec((tm, D_MODEL), lambda i: (i, 0)),
            pl.BlockSpec((tm, 2 * M_HEADS), lambda i: (i, 0)),
            pl.BlockSpec((2 * M_HEADS, tm), lambda i: (0, i)),
        ],
        out_shape=[
            jax.ShapeDtypeStruct((S, D_MODEL), BF16),
            jax.ShapeDtypeStruct((S, 2 * M_HEADS), F32),
            jax.ShapeDtypeStruct((2 * M_HEADS, S), F32),
        ],
        compiler_params=_cparams(("parallel",)),
        name="norm_gates",
    )(x2d, g, wg, bg)


def _proj_nat_kernel(h_ref, w_ref, cos_ref, sin_ref, o_ref, *, rope_start):
    j = pl.program_id(1)
    r = jnp.dot(h_ref[...], w_ref[...], preferred_element_type=F32)

    @pl.when(j < rope_start)
    def _():
        o_ref[...] = r.astype(o_ref.dtype)

    @pl.when(j >= rope_start)
    def _():
        tn = r.shape[1]
        reps = tn // LANES
        c = jnp.concatenate([cos_ref[...]] * reps, axis=1)
        s = jnp.concatenate([sin_ref[...]] * reps, axis=1)
        lane = lax.broadcasted_iota(jnp.int32, r.shape, 1)
        first = (lane % A_DH) < ROPE_HALF
        swapped = jnp.where(first, pltpu.roll(r, tn - ROPE_HALF, 1), pltpu.roll(r, ROPE_HALF, 1))
        o_ref[...] = (r * c + swapped * s).astype(o_ref.dtype)


def _proj_nat(h, w_nat, cos_full, sin_signed):
    S = h.shape[0]
    tm, tn = PROJ_TM, PROJ_TN
    return pl.pallas_call(
        functools.partial(_proj_nat_kernel, rope_start=AK_OFF // tn),
        grid=(S // tm, N_NAT // tn),
        in_specs=[
            pl.BlockSpec((tm, D_MODEL), lambda i, j: (i, 0)),
            pl.BlockSpec((D_MODEL, tn), lambda i, j: (0, j)),
            pl.BlockSpec((tm, LANES), lambda i, j: (i, 0)),
            pl.BlockSpec((tm, LANES), lambda i, j: (i, 0)),
        ],
        out_specs=pl.BlockSpec((tm, tn), lambda i, j: (i, j)),
        out_shape=jax.ShapeDtypeStruct((S, N_NAT), BF16),
        compiler_params=_cparams(("parallel", "arbitrary")),
        name="proj_nat",
    )(h, w_nat, cos_full, sin_signed)


def _proj_t_kernel(h_ref, wt_ref, cos_ref, sin_ref, o_ref):
    h = h_ref[...]
    cT = cos_ref[...]
    sT = sin_ref[...]
    nblk = o_ref.shape[0]
    rows = 256
    scale = A_DH ** -0.5 * LOG2E
    for rb in range(wt_ref.shape[0] // rows):
        r = lax.dot_general(wt_ref[rb * rows:(rb + 1) * rows, :], h,
                            (((1,), (1,)), ((), ())), preferred_element_type=F32)
        if rb * rows < A_WIDTH:
            parts = []
            for g in range(rows // A_DH):
                t1 = r[g * A_DH:g * A_DH + ROPE_HALF]
                t2 = r[g * A_DH + ROPE_HALF:(g + 1) * A_DH]
                parts.append((t1 * cT - t2 * sT) * scale)
                parts.append((t2 * cT + t1 * sT) * scale)
            r = jnp.concatenate(parts, axis=0)
        r = r.astype(o_ref.dtype)
        for b in range(nblk):
            o_ref[b, rb * rows:(rb + 1) * rows, :] = r[:, b * ATT_BLK:(b + 1) * ATT_BLK]


def _proj_t(h, wt, cosT, sinT):
    S = h.shape[0]
    tm = PROJT_TM
    nblk = tm // ATT_BLK
    nfeat = wt.shape[0]
    return pl.pallas_call(
        _proj_t_kernel,
        grid=(S // tm,),
        in_specs=[
            pl.BlockSpec((tm, D_MODEL), lambda i: (i, 0)),
            pl.BlockSpec((nfeat, D_MODEL), lambda i: (0, 0)),
            pl.BlockSpec((ROPE_HALF, tm), lambda i: (0, i)),
            pl.BlockSpec((ROPE_HALF, tm), lambda i: (0, i)),
        ],
        out_specs=pl.BlockSpec((nblk, nfeat, ATT_BLK), lambda i: (i, 0, 0)),
        out_shape=jax.ShapeDtypeStruct((S // ATT_BLK, nfeat, ATT_BLK), BF16),
        compiler_params=_cparams(("parallel",)),
        name="proj_t",
    )(h, wt, cosT, sinT)


def _split3(x):
    hi = x.astype(BF16)
    r1 = x - hi.astype(F32)
    mid = r1.astype(BF16)
    lo = (r1 - mid.astype(F32)).astype(BF16)
    return hi, mid, lo


def _mlstm_kernel(pm_ref, gc_ref, gr_ref, cw_ref, cb_ref, mg_ref, o_ref,
                  xbuf, c_ref, m_ref):
    R = pm_ref.shape[0]
    L = ML_L
    n_chunks = R // L
    NQK = 2 * N_MQK

    @pl.when(pl.program_id(0) == 0)
    def _():
        xbuf[0:8, :] = jnp.zeros((8, NQK), F32)
        c_ref[...] = jnp.zeros_like(c_ref)
        m_ref[...] = jnp.zeros_like(m_ref)

    x = pm_ref[:, 0:NQK].astype(F32)
    xbuf[8:8 + R, :] = x
    y = cb_ref[...]
    y = y + xbuf[5:5 + R, :] * cw_ref[0:1, :]
    y = y + xbuf[6:6 + R, :] * cw_ref[1:2, :]
    y = y + xbuf[7:7 + R, :] * cw_ref[2:3, :]
    y = y + x * cw_ref[3:4, :]
    xbuf[0:8, :] = x[R - 8:R, :]
    qk = y * jax.nn.sigmoid(y)
    q_all = (qk[:, :N_MQK] * (M_DQK ** -0.5)).astype(BF16)
    k_all = qk[:, N_MQK:]

    t_idx = lax.broadcasted_iota(jnp.int32, (L, L), 0)
    s_idx = lax.broadcasted_iota(jnp.int32, (L, L), 1)
    causal = s_idx <= t_idx
    tril = causal.astype(BF16)
    triu = (t_idx <= s_idx).astype(BF16)
    ones_ext = jnp.ones((L, ONES_W), BF16)

    for c in range(n_chunks):
        rows = slice(c * L, (c + 1) * L)
        gc = gc_ref[rows, :]
        gr = gr_ref[:, rows]
        b_col_all = sum(jnp.dot(tril, p, preferred_element_type=F32) for p in _split3(gc))
        b_row_all = sum(jnp.dot(p, triu, preferred_element_type=F32) for p in _split3(gr))
        for hd in range(M_HEADS):
            qh = q_all[rows, hd * M_DQK:(hd + 1) * M_DQK]
            kh = k_all[rows, hd * M_DQK:(hd + 1) * M_DQK]
            vh = pm_ref[rows, NQK + hd * M_DV:NQK + (hd + 1) * M_DV]
            v_ext = jnp.concatenate([vh, ones_ext], axis=1)
            i_row = gr[hd:hd + 1, :]
            b_row = b_row_all[M_HEADS + hd:M_HEADS + hd + 1, :]
            b_col = b_col_all[:, M_HEADS + hd:M_HEADS + hd + 1]

            logd = jnp.where(causal, b_col - b_row + i_row, -jnp.inf)
            m_loc = jnp.max(logd, axis=1, keepdims=True)
            d_loc = jnp.exp(logd - m_loc)
            s = lax.dot_general(qh, kh.astype(BF16), (((1,), (1,)), ((), ())),
                                preferred_element_type=F32)
            p_ext = jnp.dot((s * d_loc).astype(BF16), v_ext, preferred_element_type=F32)

            b_last = b_row[:, L - 1:L]
            w_log = b_last - b_row + i_row
            a_loc = jnp.max(w_log, axis=1, keepdims=True)
            kw_t = (kh.T * jnp.exp(w_log - a_loc)).astype(BF16)
            kv_ext = jnp.dot(kw_t, v_ext, preferred_element_type=F32)

            m_prev = m_ref[hd:hd + 1, 0:1]
            c_prev = c_ref[hd]
            inter = b_col + m_prev
            m_t = jnp.maximum(inter, m_loc)
            qc = jnp.dot(qh, c_prev.astype(BF16), preferred_element_type=F32)
            nd = jnp.exp(inter - m_t) * qc + jnp.exp(m_loc - m_t) * p_ext
            num = nd[:, :M_DV]
            den = nd[:, M_DV:M_DV + 1]
            hh = num / jnp.maximum(jnp.abs(den), jnp.exp(-m_t))

            m_new = jnp.maximum(b_last + m_prev, a_loc)
            c_ref[hd] = jnp.exp(b_last + m_prev - m_new) * c_prev + jnp.exp(a_loc - m_new) * kv_ext
            m_ref[hd:hd + 1, :] = jnp.broadcast_to(m_new, (1, m_ref.shape[1]))

            cols = slice(hd * M_DV, (hd + 1) * M_DV)
            ms = jnp.mean(hh * hh, axis=-1, keepdims=True)
            hn = (hh * lax.rsqrt(ms + EPS)) * mg_ref[:, cols]
            mo = pm_ref[rows, NQK + M_WIDTH + hd * M_DV:NQK + M_WIDTH + (hd + 1) * M_DV].astype(F32)
            o_ref[rows, cols] = (hn * jax.nn.sigmoid(mo)).astype(o_ref.dtype)


def _mlstm(pn, gc, gr, conv_w, conv_b, mnorm_g):
    S = pn.shape[0]
    R = ML_R
    wm = 2 * N_MQK + 2 * M_WIDTH
    return pl.pallas_call(
        _mlstm_kernel,
        grid=(S // R,),
        in_specs=[
            pl.BlockSpec((R, wm), lambda i: (i, 0)),
            pl.BlockSpec((R, 2 * M_HEADS), lambda i: (i, 0)),
            pl.BlockSpec((2 * M_HEADS, R), lambda i: (0, i)),
            pl.BlockSpec((M_CONV, 2 * N_MQK), lambda i: (0, 0)),
            pl.BlockSpec((1, 2 * N_MQK), lambda i: (0, 0)),
            pl.BlockSpec((1, M_WIDTH), lambda i: (0, 0)),
        ],
        out_specs=pl.BlockSpec((R, M_WIDTH), lambda i: (i, 0)),
        out_shape=jax.ShapeDtypeStruct((S, M_WIDTH), BF16),
        scratch_shapes=[
            pltpu.VMEM((R + 8, 2 * N_MQK), F32),
            pltpu.VMEM((M_HEADS, M_DQK, M_DV + ONES_W), F32),
            pltpu.VMEM((8, LANES), F32),
        ],
        compiler_params=_cparams(("arbitrary",)),
        name="mlstm",
    )(pn, gc, gr, conv_w, conv_b, mnorm_g)


def _attn_kernel(qT_ref, k_ref, vT_ref, lam_ref, g_ref, o_ref,
                 qz_ref, p_ref, m_ref, l_ref, acc_ref, *, lam_init):
    qi = pl.program_id(1)
    tq, tk, blk = ATT_TQ, ATT_TK, ATT_BLK
    n_blocks = (qi + 1) * (tq // tk)
    ncol = 2 * tq
    wa = 256

    frow = lax.broadcasted_iota(jnp.int32, (2 * A_DH, blk), 0)
    for b in range(tq // blk):
        qb = qT_ref[b]
        zero = jnp.zeros_like(qb)
        qz_ref[:, b * blk:(b + 1) * blk] = jnp.where(frow < A_DH, qb, zero)
        qz_ref[:, tq + b * blk:tq + (b + 1) * blk] = jnp.where(frow >= A_DH, qb, zero)

    def key_block(t):
        return k_ref[pl.ds(pl.multiple_of(t * tk, tk), tk), :]

    def diag_visible(c, d):
        kc = (lax.broadcasted_iota(jnp.int32, (tk, wa), 0) + d * tk) // CHUNK
        qc = ((lax.broadcasted_iota(jnp.int32, (tk, wa), 1) + c * wa) % tq) // CHUNK
        return kc <= qc

    def scores(kb, c):
        return jnp.dot(kb, qz_ref[:, c * wa:(c + 1) * wa], preferred_element_type=F32)

    kb0 = key_block(0)
    for c in range(ncol // wa):
        cols = slice(c * wa, (c + 1) * wa)
        vis = jnp.logical_or(diag_visible(c, 0), qi > 0)
        m_ref[:, cols] = jnp.max(jnp.where(vis, scores(kb0, c), -jnp.inf), axis=0, keepdims=True)
    l_ref[...] = jnp.zeros_like(l_ref)
    acc_ref[...] = jnp.zeros_like(acc_ref)
    p_ref[1] = jnp.zeros(p_ref.shape[1:], p_ref.dtype)

    def stage_ab(t, par, diag=None):
        kb = key_block(t)
        for c in range(ncol // wa):
            cols = slice(c * wa, (c + 1) * wa)
            s = scores(kb, c)
            if diag is not None:
                s = jnp.where(diag_visible(c, diag), s, -jnp.inf)
            p = jnp.exp2(s - m_ref[:, cols])
            l_ref[:, cols] += jnp.sum(p, axis=0, keepdims=True)
            p_ref[par, :, cols] = p.astype(BF16)

    def stage_c(t, par):
        vb = vT_ref[t]
        for c in range(ncol // wa):
            cols = slice(c * wa, (c + 1) * wa)
            acc_ref[:, cols] += jnp.dot(vb, p_ref[par, :, cols], preferred_element_type=F32)

    def pair(u, carry):
        t = 2 * u
        stage_c(jnp.maximum(t - 1, 0), 1)
        stage_ab(t, 0)
        stage_c(t, 0)
        stage_ab(t + 1, 1)
        return carry

    lax.fori_loop(0, qi, pair, 0)
    t = 2 * qi
    stage_c(jnp.maximum(t - 1, 0), 1)
    stage_ab(t, 0, diag=0)
    stage_c(t, 0)
    stage_ab(t + 1, 1, diag=1)
    stage_c(t + 1, 1)

    l_fast = l_ref[...]
    overflow = jnp.max(jnp.where(l_fast < ATT_L_LIMIT, 0.0, 1.0)) > 0.5

    @pl.when(overflow)
    def _():
        m_ref[...] = jnp.full_like(m_ref, -jnp.inf)
        l_ref[...] = jnp.zeros_like(l_ref)
        acc_ref[...] = jnp.zeros_like(acc_ref)

        def block(t, carry):
            kb = key_block(t)
            vb = vT_ref[t]
            for c in range(ncol // wa):
                cols = slice(c * wa, (c + 1) * wa)
                kc = (lax.broadcasted_iota(jnp.int32, (tk, wa), 0) + t * tk) // CHUNK
                qc = ((lax.broadcasted_iota(jnp.int32, (tk, wa), 1) + c * wa) % tq + qi * tq) // CHUNK
                s = jnp.where(kc <= qc, scores(kb, c), -jnp.inf)
                m_old = m_ref[:, cols]
                m_new = jnp.maximum(m_old, jnp.max(s, axis=0, keepdims=True))
                alpha = jnp.exp2(m_old - m_new)
                p = jnp.exp2(s - m_new)
                l_ref[:, cols] = alpha * l_ref[:, cols] + jnp.sum(p, axis=0, keepdims=True)
                m_ref[:, cols] = m_new
                acc_ref[:, cols] = alpha * acc_ref[:, cols] + jnp.dot(
                    vb, p.astype(BF16), preferred_element_type=F32)
            return carry

        lax.fori_loop(0, n_blocks, block, 0)

    lam_p = lam_ref[...]
    lam = (jnp.exp(jnp.sum(lam_p[0:1] * lam_p[1:2], axis=1, keepdims=True))
           - jnp.exp(jnp.sum(lam_p[2:3] * lam_p[3:4], axis=1, keepdims=True)) + lam_init)
    acc = acc_ref[...]
    l = l_ref[...]
    out = acc[:, :tq] / l[:, :tq] - lam * (acc[:, tq:] / l[:, tq:])
    ms = jnp.mean(out * out, axis=0, keepdims=True)
    y = ((out * lax.rsqrt(ms + EPS)) * g_ref[...]) * (1.0 - lam_init)
    o_ref[...] = y.T.astype(o_ref.dtype)


def _attn(qvT, pn, lam_params, subln_col, lam_init):
    nblk = qvT.shape[0]
    S = pn.shape[0]
    tq, tk, blk = ATT_TQ, ATT_TK, ATT_BLK
    assert tk == blk and tq == 2 * tk
    k_col0 = AK_OFF // (2 * A_DH)
    return pl.pallas_call(
        functools.partial(_attn_kernel, lam_init=lam_init),
        grid=(A_HEADS, S // tq),
        in_specs=[
            pl.BlockSpec((tq // blk, 2 * A_DH, blk), lambda h, i: (i, h, 0)),
            pl.BlockSpec((S, 2 * A_DH), lambda h, i: (0, k_col0 + h)),
            pl.BlockSpec((nblk, A_DV, blk), lambda h, i: (0, A_HEADS + h, 0)),
            pl.BlockSpec((4, A_DH), lambda h, i: (0, 0)),
            pl.BlockSpec((A_DV, 1), lambda h, i: (0, 0)),
        ],
        out_specs=pl.BlockSpec((tq, A_DV), lambda h, i: (i, h)),
        out_shape=jax.ShapeDtypeStruct((S, A_WIDTH), BF16),
        scratch_shapes=[
            pltpu.VMEM((2 * A_DH, 2 * tq), BF16),
            pltpu.VMEM((2, tk, 2 * tq), BF16),
            pltpu.VMEM((1, 2 * tq), F32),
            pltpu.VMEM((1, 2 * tq), F32),
            pltpu.VMEM((A_DV, 2 * tq), F32),
        ],
        compiler_params=_cparams(("parallel", "arbitrary")),
        name="diff_attn",
    )(qvT, pn, qvT, lam_params, subln_col)


def _out_proj_kernel(hm_ref, ha_ref, w_ref, x_ref, g_ref, x1_ref, h2_ref):
    r = jnp.dot(hm_ref[...], w_ref[0:M_WIDTH, :], preferred_element_type=F32)
    r = r + jnp.dot(ha_ref[...], w_ref[M_WIDTH:, :], preferred_element_type=F32)
    x1 = x_ref[...] + r
    x1_ref[...] = x1
    ms = jnp.mean(x1 * x1, axis=-1, keepdims=True)
    h2_ref[...] = ((x1 * lax.rsqrt(ms + EPS)) * g_ref[...]).astype(h2_ref.dtype)


def _out_proj(hm, ha, w_out, x2d, g):
    S = x2d.shape[0]
    tm = OUT_TM
    return pl.pallas_call(
        _out_proj_kernel,
        grid=(S // tm,),
        in_specs=[
            pl.BlockSpec((tm, M_WIDTH), lambda i: (i, 0)),
            pl.BlockSpec((tm, A_WIDTH), lambda i: (i, 0)),
            pl.BlockSpec((M_WIDTH + A_WIDTH, D_MODEL), lambda i: (0, 0)),
            pl.BlockSpec((tm, D_MODEL), lambda i: (i, 0)),
            pl.BlockSpec((1, D_MODEL), lambda i: (0, 0)),
        ],
        out_specs=[
            pl.BlockSpec((tm, D_MODEL), lambda i: (i, 0)),
            pl.BlockSpec((tm, D_MODEL), lambda i: (i, 0)),
        ],
        out_shape=[
            jax.ShapeDtypeStruct((S, D_MODEL), F32),
            jax.ShapeDtypeStruct((S, D_MODEL), BF16),
        ],
        compiler_params=_cparams(("parallel",)),
        name="out_proj",
    )(hm, ha, w_out, x2d, g)


def _swiglu_kernel(h2_ref, wg_ref, wu_ref, wd_ref, x1_ref, fg_ref, o_ref, acc_ref):
    j = pl.program_id(1)

    @pl.when(j == 0)
    def _():
        acc_ref[...] = jnp.zeros_like(acc_ref)

    h2 = h2_ref[...]
    g = jnp.dot(h2, wg_ref[...], preferred_element_type=F32)
    u = jnp.dot(h2, wu_ref[...], preferred_element_type=F32)
    a = ((g * jax.nn.sigmoid(g)) * u).astype(BF16)
    acc_ref[...] += jnp.dot(a, wd_ref[...], preferred_element_type=F32)

    @pl.when(j == pl.num_programs(1) - 1)
    def _():
        y = x1_ref[...] + acc_ref[...]
        ms = jnp.mean(y * y, axis=-1, keepdims=True)
        o_ref[...] = (y * lax.rsqrt(ms + EPS)) * fg_ref[...]


def _swiglu(h2, wg, wu, wd, x1, fg):
    S = h2.shape[0]
    tm, tf = FF_TM, FF_TF
    return pl.pallas_call(
        _swiglu_kernel,
        grid=(S // tm, D_FF // tf),
        in_specs=[
            pl.BlockSpec((tm, D_MODEL), lambda i, j: (i, 0)),
            pl.BlockSpec((D_MODEL, tf), lambda i, j: (0, j)),
            pl.BlockSpec((D_MODEL, tf), lambda i, j: (0, j)),
            pl.BlockSpec((tf, D_MODEL), lambda i, j: (j, 0)),
            pl.BlockSpec((tm, D_MODEL), lambda i, j: (i, 0)),
            pl.BlockSpec((1, D_MODEL), lambda i, j: (0, 0)),
        ],
        out_specs=pl.BlockSpec((tm, D_MODEL), lambda i, j: (i, 0)),
        out_shape=jax.ShapeDtypeStruct((S, D_MODEL), F32),
        scratch_shapes=[pltpu.VMEM((tm, D_MODEL), F32)],
        compiler_params=_cparams(("parallel", "arbitrary")),
        name="swiglu",
    )(h2, wg, wu, wd, x1, fg)


def _rope_tables(S):
    inv = ROPE_THETA ** (-jnp.arange(ROPE_HALF, dtype=F32) / ROPE_HALF)
    ang = jnp.arange(S).astype(F32)[:, None] * inv[None, :]
    cos = jnp.cos(ang)
    sin = jnp.sin(ang)
    cos_full = jnp.tile(cos, (1, LANES // ROPE_HALF))
    sin_signed = jnp.tile(jnp.concatenate([-sin, sin], axis=1), (1, LANES // A_DH))
    return cos_full, sin_signed, cos.T, sin.T


def kernel(x, norm1_g, w_in, conv_w, conv_b, b_igate, b_fgate, mnorm_g,
           lambda_q1, lambda_k1, lambda_q2, lambda_k2, subln_g, w_out,
           norm2_g, w_gate, w_up, w_down, final_g):
    B, S, D = x.shape
    assert (B, S, D) == (1, SEQ, D_MODEL) and norm1_g.shape[0] == 1
    l = 0
    x2d = x.reshape(S, D)

    w = w_in[l]
    o_mi = 2 * N_MQK + 2 * M_WIDTH
    o_aq = o_mi + 2 * M_HEADS
    o_ak = o_aq + A_WIDTH
    o_av = o_ak + A_WIDTH
    w_nat = jnp.concatenate([w[:, :o_mi], w[:, o_ak:o_av]], axis=1).astype(BF16)
    w_t = jnp.concatenate([w[:, o_aq:o_ak], w[:, o_av:]], axis=1).T.astype(BF16)
    w_g = jnp.pad(w[:, o_mi:o_aq], ((0, 0), (0, LANES - 2 * M_HEADS))).astype(BF16)
    b_g = jnp.pad(jnp.concatenate([b_igate[l], b_fgate[l]]), (0, LANES - 2 * M_HEADS)).reshape(1, LANES)

    cos_full, sin_signed, cosT, sinT = _rope_tables(S)

    h, gc, gr = _norm_gates(x2d, norm1_g[l].reshape(1, D), w_g, b_g)
    pn = _proj_nat(h, w_nat, cos_full, sin_signed)
    qvT = _proj_t(h, w_t, cosT, sinT)

    hm = _mlstm(pn, gc, gr, conv_w[l], conv_b[l].reshape(1, -1), mnorm_g[l].reshape(1, M_WIDTH))

    lam_init = 0.8 - 0.6 * math.exp(-0.3 * l)
    lam_params = jnp.stack([lambda_q1[l], lambda_k1[l], lambda_q2[l], lambda_k2[l]]).astype(F32)
    ha = _attn(qvT, pn, lam_params, subln_g[l].reshape(A_DV, 1), lam_init)

    x1, h2 = _out_proj(hm, ha, w_out[l].astype(BF16), x2d, norm2_g[l].reshape(1, D))
    out = _swiglu(h2, w_gate[l].astype(BF16), w_up[l].astype(BF16), w_down[l].astype(BF16),
                  x1, final_g.reshape(1, D))
    return out.reshape(B, S, D)
```

```python
import functools
import math

import jax
import jax.numpy as jnp
from jax import lax
from jax.experimental import pallas as pl
from jax.experimental.pallas import tpu as pltpu

F32 = jnp.float32
BF16 = jnp.bfloat16

D_MODEL = 2048
SEQ = 16384
CHUNK = 64
EPS = 1e-6
ROPE_THETA = 10000.0

M_HEADS = 4
M_DV = 256
M_DQK = 128
M_CONV = 4
GATE_CAP = 15.0
M_WIDTH = M_HEADS * M_DV
N_MQK = M_HEADS * M_DQK

A_HEADS = 8
A_DH = 64
A_DV = 128
A_WIDTH = A_HEADS * A_DV
ROPE_HALF = A_DH // 2

D_FF = 5632

LANES = 128
VMEM_LIMIT = 56 * 1024 * 1024

FRONT_TM = 512
FRONT_TN = 512
ATT_BLK = 512
ATT_TQ = 1024
ATT_TK = 512
LOG2E = math.log2(math.e)
ATT_L_LIMIT = 2.0 ** 60
ML_L = 128
ML_R = 512
OUT_TM = 512
FF_TM = 512
FF_TF = 512

N_NAT = 2 * N_MQK + 2 * M_WIDTH + A_WIDTH
AK_OFF = 2 * N_MQK + 2 * M_WIDTH
ONES_W = LANES


def _cparams(sem):
    return pltpu.CompilerParams(dimension_semantics=sem, vmem_limit_bytes=VMEM_LIMIT)


def _front_kernel(x_ref, g_ref, wg_ref, bg_ref, wn_ref, wt_ref, cos_ref, sin_ref, cosT_ref, sinT_ref,
                  gc_ref, gr_ref, pn_ref, qv_ref, h_ref):
    x = x_ref[...]
    ms = jnp.mean(x * x, axis=-1, keepdims=True)
    h_ref[...] = ((x * lax.rsqrt(ms + EPS)) * g_ref[...]).astype(BF16)
    h = h_ref[...]

    pre = jnp.dot(h, wg_ref[...], preferred_element_type=F32) + bg_ref[...]
    capped = GATE_CAP * jnp.tanh(pre / GATE_CAP)
    logf = -(jnp.maximum(-capped, 0.0) + jnp.log1p(jnp.exp(-jnp.abs(capped))))
    lane = lax.broadcasted_iota(jnp.int32, pre.shape, 1)
    gates = jnp.where(lane < M_HEADS, capped, logf)
    gc_ref[...] = gates[:, :2 * M_HEADS]
    gr_ref[...] = gates.T[:2 * M_HEADS, :]

    tn = FRONT_TN
    reps = tn // LANES
    for j in range(N_NAT // tn):
        r = jnp.dot(h, wn_ref[:, j * tn:(j + 1) * tn], preferred_element_type=F32)
        if j * tn >= AK_OFF:
            c = jnp.concatenate([cos_ref[...]] * reps, axis=1)
            s = jnp.concatenate([sin_ref[...]] * reps, axis=1)
            ln = lax.broadcasted_iota(jnp.int32, r.shape, 1)
            first = (ln % A_DH) < ROPE_HALF
            swapped = jnp.where(first, pltpu.roll(r, tn - ROPE_HALF, 1), pltpu.roll(r, ROPE_HALF, 1))
            r = r * c + swapped * s
        pn_ref[:, j * tn:(j + 1) * tn] = r.astype(pn_ref.dtype)

    cT = cosT_ref[...]
    sT = sinT_ref[...]
    rows = 256
    scale = A_DH ** -0.5 * LOG2E
    for rb in range(wt_ref.shape[0] // rows):
        r = lax.dot_general(wt_ref[rb * rows:(rb + 1) * rows, :], h,
                            (((1,), (1,)), ((), ())), preferred_element_type=F32)
        if rb * rows < A_WIDTH:
            parts = []
            for g in range(rows // A_DH):
                t1 = r[g * A_DH:g * A_DH + ROPE_HALF]
                t2 = r[g * A_DH + ROPE_HALF:(g + 1) * A_DH]
                parts.append((t1 * cT - t2 * sT) * scale)
                parts.append((t2 * cT + t1 * sT) * scale)
            r = jnp.concatenate(parts, axis=0)
        qv_ref[0, rb * rows:(rb + 1) * rows, :] = r.astype(qv_ref.dtype)


def _front(x2d, g, wg, bg, w_nat, w_t, cos_full, sin_signed, cosT, sinT):
    S = x2d.shape[0]
    tm = FRONT_TM
    assert tm == ATT_BLK
    nfeat = w_t.shape[0]
    const = lambda i: (0, 0)
    resident = dict(pipeline_mode=pl.Buffered(1))
    return pl.pallas_call(
        _front_kernel,
        grid=(S // tm,),
        in_specs=[
            pl.BlockSpec((tm, D_MODEL), lambda i: (i, 0)),
            pl.BlockSpec((1, D_MODEL), const),
            pl.BlockSpec((D_MODEL, LANES), const),
            pl.BlockSpec((1, LANES), const),
            pl.BlockSpec((D_MODEL, N_NAT), const, **resident),
            pl.BlockSpec((nfeat, D_MODEL), const, **resident),
            pl.BlockSpec((tm, LANES), lambda i: (i, 0)),
            pl.BlockSpec((tm, LANES), lambda i: (i, 0)),
            pl.BlockSpec((ROPE_HALF, tm), lambda i: (0, i)),
            pl.BlockSpec((ROPE_HALF, tm), lambda i: (0, i)),
        ],
        out_specs=[
            pl.BlockSpec((tm, 2 * M_HEADS), lambda i: (i, 0)),
            pl.BlockSpec((2 * M_HEADS, tm), lambda i: (0, i)),
            pl.BlockSpec((tm, N_NAT), lambda i: (i, 0)),
            pl.BlockSpec((1, nfeat, ATT_BLK), lambda i: (i, 0, 0)),
        ],
        out_shape=[
            jax.ShapeDtypeStruct((S, 2 * M_HEADS), F32),
            jax.ShapeDtypeStruct((2 * M_HEADS, S), F32),
            jax.ShapeDtypeStruct((S, N_NAT), BF16),
            jax.ShapeDtypeStruct((S // ATT_BLK, nfeat, ATT_BLK), BF16),
        ],
        scratch_shapes=[pltpu.VMEM((tm, D_MODEL), BF16)],
        compiler_params=_cparams(("parallel",)),
        name="front",
    )(x2d, g, wg, bg, w_nat, w_t, cos_full, sin_signed, cosT, sinT)


def _split3(x):
    hi = x.astype(BF16)
    r1 = x - hi.astype(F32)
    mid = r1.astype(BF16)
    lo = (r1 - mid.astype(F32)).astype(BF16)
    return hi, mid, lo


def _mlstm_kernel(pm_ref, gc_ref, gr_ref, cw_ref, cb_ref, mg_ref, o_ref,
                  xbuf, c_ref, m_ref):
    R = pm_ref.shape[0]
    L = ML_L
    n_chunks = R // L
    NQK = 2 * N_MQK

    @pl.when(pl.program_id(0) == 0)
    def _():
        xbuf[0:8, :] = jnp.zeros((8, NQK), F32)
        c_ref[...] = jnp.zeros_like(c_ref)
        m_ref[...] = jnp.zeros_like(m_ref)

    x = pm_ref[:, 0:NQK].astype(F32)
    xbuf[8:8 + R, :] = x
    y = cb_ref[...]
    y = y + xbuf[5:5 + R, :] * cw_ref[0:1, :]
    y = y + xbuf[6:6 + R, :] * cw_ref[1:2, :]
    y = y + xbuf[7:7 + R, :] * cw_ref[2:3, :]
    y = y + x * cw_ref[3:4, :]
    xbuf[0:8, :] = x[R - 8:R, :]
    qk = y * jax.nn.sigmoid(y)
    q_all = (qk[:, :N_MQK] * (M_DQK ** -0.5)).astype(BF16)
    k_all = qk[:, N_MQK:]

    t_idx = lax.broadcasted_iota(jnp.int32, (L, L), 0)
    s_idx = lax.broadcasted_iota(jnp.int32, (L, L), 1)
    causal = s_idx <= t_idx
    tril = causal.astype(BF16)
    triu = (t_idx <= s_idx).astype(BF16)
    ones_ext = jnp.ones((L, ONES_W), BF16)

    for c in range(n_chunks):
        rows = slice(c * L, (c + 1) * L)
        gc = gc_ref[rows, :]
        gr = gr_ref[:, rows]
        b_col_all = sum(jnp.dot(tril, p, preferred_element_type=F32) for p in _split3(gc))
        b_row_all = sum(jnp.dot(p, triu, preferred_element_type=F32) for p in _split3(gr))
        for hd in range(M_HEADS):
            qh = q_all[rows, hd * M_DQK:(hd + 1) * M_DQK]
            kh = k_all[rows, hd * M_DQK:(hd + 1) * M_DQK]
            vh = pm_ref[rows, NQK + hd * M_DV:NQK + (hd + 1) * M_DV]
            v_ext = jnp.concatenate([vh, ones_ext], axis=1)
            i_row = gr[hd:hd + 1, :]
            b_row = b_row_all[M_HEADS + hd:M_HEADS + hd + 1, :]
            b_col = b_col_all[:, M_HEADS + hd:M_HEADS + hd + 1]

            logd = jnp.where(causal, b_col - b_row + i_row, -jnp.inf)
            m_loc = jnp.max(logd, axis=1, keepdims=True)
            d_loc = jnp.exp(logd - m_loc)
            s = lax.dot_general(qh, kh.astype(BF16), (((1,), (1,)), ((), ())),
                                preferred_element_type=F32)
            p_ext = jnp.dot((s * d_loc).astype(BF16), v_ext, preferred_element_type=F32)

            b_last = b_row[:, L - 1:L]
            w_log = b_last - b_row + i_row
            a_loc = jnp.max(w_log, axis=1, keepdims=True)
            kw_t = (kh.T * jnp.exp(w_log - a_loc)).astype(BF16)
            kv_ext = jnp.dot(kw_t, v_ext, preferred_element_type=F32)

            m_prev = m_ref[hd:hd + 1, 0:1]
            c_prev = c_ref[hd]
            inter = b_col + m_prev
            m_t = jnp.maximum(inter, m_loc)
            qc = jnp.dot(qh, c_prev.astype(BF16), preferred_element_type=F32)
            nd = jnp.exp(inter - m_t) * qc + jnp.exp(m_loc - m_t) * p_ext
            num = nd[:, :M_DV]
            den = nd[:, M_DV:M_DV + 1]
            hh = num / jnp.maximum(jnp.abs(den), jnp.exp(-m_t))

            m_new = jnp.maximum(b_last + m_prev, a_loc)
            c_ref[hd] = jnp.exp(b_last + m_prev - m_new) * c_prev + jnp.exp(a_loc - m_new) * kv_ext
            m_ref[hd:hd + 1, :] = jnp.broadcast_to(m_new, (1, m_ref.shape[1]))

            cols = slice(hd * M_DV, (hd + 1) * M_DV)
            ms = jnp.mean(hh * hh, axis=-1, keepdims=True)
            hn = (hh * lax.rsqrt(ms + EPS)) * mg_ref[:, cols]
            mo = pm_ref[rows, NQK + M_WIDTH + hd * M_DV:NQK + M_WIDTH + (hd + 1) * M_DV].astype(F32)
            o_ref[rows, cols] = (hn * jax.nn.sigmoid(mo)).astype(o_ref.dtype)


def _mlstm(pn, gc, gr, conv_w, conv_b, mnorm_g):
    S = pn.shape[0]
    R = ML_R
    wm = 2 * N_MQK + 2 * M_WIDTH
    return pl.pallas_call(
        _mlstm_kernel,
        grid=(S // R,),
        in_specs=[
            pl.BlockSpec((R, wm), lambda i: (i, 0)),
            pl.BlockSpec((R, 2 * M_HEADS), lambda i: (i, 0)),
            pl.BlockSpec((2 * M_HEADS, R), lambda i: (0, i)),
            pl.BlockSpec((M_CONV, 2 * N_MQK), lambda i: (0, 0)),
            pl.BlockSpec((1, 2 * N_MQK), lambda i: (0, 0)),
            pl.BlockSpec((1, M_WIDTH), lambda i: (0, 0)),
        ],
        out_specs=pl.BlockSpec((R, M_WIDTH), lambda i: (i, 0)),
        out_shape=jax.ShapeDtypeStruct((S, M_WIDTH), BF16),
        scratch_shapes=[
            pltpu.VMEM((R + 8, 2 * N_MQK), F32),
            pltpu.VMEM((M_HEADS, M_DQK, M_DV + ONES_W), F32),
            pltpu.VMEM((8, LANES), F32),
        ],
        compiler_params=_cparams(("arbitrary",)),
        name="mlstm",
    )(pn, gc, gr, conv_w, conv_b, mnorm_g)


def _attn_kernel(qT_ref, k_ref, vT_ref, lam_ref, g_ref, o_ref,
                 qz_ref, s0_ref, p_ref, m_ref, l_ref, acc_ref, *, lam_init):
    qi = pl.program_id(1)
    tq, tk, blk = ATT_TQ, ATT_TK, ATT_BLK
    n_blocks = (qi + 1) * (tq // tk)
    ncol = 2 * tq
    wa = 256

    frow = lax.broadcasted_iota(jnp.int32, (2 * A_DH, blk), 0)
    for b in range(tq // blk):
        qb = qT_ref[b]
        zero = jnp.zeros_like(qb)
        qz_ref[:, b * blk:(b + 1) * blk] = jnp.where(frow < A_DH, qb, zero)
        qz_ref[:, tq + b * blk:tq + (b + 1) * blk] = jnp.where(frow >= A_DH, qb, zero)

    def key_block(t):
        return k_ref[pl.ds(pl.multiple_of(t * tk, tk), tk), :]

    def diag_visible(c, d):
        kc = (lax.broadcasted_iota(jnp.int32, (tk, wa), 0) + d * tk) // CHUNK
        qc = ((lax.broadcasted_iota(jnp.int32, (tk, wa), 1) + c * wa) % tq) // CHUNK
        return kc <= qc

    def scores(kb, c):
        return jnp.dot(kb, qz_ref[:, c * wa:(c + 1) * wa], preferred_element_type=F32)

    all_chunks = tuple(range(ncol // wa))
    late_chunks = tuple(c for c in all_chunks if (c * wa) % tq >= tk)

    kb0 = key_block(0)
    for c in all_chunks:
        cols = slice(c * wa, (c + 1) * wa)
        vis = jnp.logical_or(diag_visible(c, 0), qi > 0)
        s = jnp.where(vis, scores(kb0, c), -jnp.inf)
        s0_ref[:, cols] = s
        m_ref[:, cols] = jnp.max(s, axis=0, keepdims=True)
    for c in all_chunks:
        cols = slice(c * wa, (c + 1) * wa)
        p = jnp.exp2(s0_ref[:, cols] - m_ref[:, cols])
        l_ref[:, cols] = jnp.sum(p, axis=0, keepdims=True)
        p_ref[0, :, cols] = p.astype(BF16)
    acc_ref[...] = jnp.zeros_like(acc_ref)

    def stage_ab(t, par, diag=None, chunks=all_chunks):
        kb = key_block(t)
        for c in chunks:
            cols = slice(c * wa, (c + 1) * wa)
            s = scores(kb, c)
            if diag is not None:
                s = jnp.where(diag_visible(c, diag), s, -jnp.inf)
            p = jnp.exp2(s - m_ref[:, cols])
            l_ref[:, cols] += jnp.sum(p, axis=0, keepdims=True)
            p_ref[par, :, cols] = p.astype(BF16)

    def stage_c(t, par, chunks=all_chunks):
        vb = vT_ref[t]
        for c in chunks:
            cols = slice(c * wa, (c + 1) * wa)
            acc_ref[:, cols] += jnp.dot(vb, p_ref[par, :, cols], preferred_element_type=F32)

    @pl.when(qi > 0)
    def _():
        def pair(u, carry):
            t = 2 * u
            stage_c(t, 0)
            stage_ab(t + 1, 1)
            stage_c(t + 1, 1)
            stage_ab(t + 2, 0)
            return carry

        lax.fori_loop(0, qi - 1, pair, 0)
        t = 2 * qi
        stage_c(t - 2, 0)
        stage_ab(t - 1, 1)
        stage_c(t - 1, 1)
        stage_ab(t, 0, diag=0)

    t = 2 * qi
    stage_c(t, 0)
    stage_ab(t + 1, 1, diag=1, chunks=late_chunks)
    stage_c(t + 1, 1, chunks=late_chunks)

    l_fast = l_ref[...]
    overflow = jnp.max(jnp.where(l_fast < ATT_L_LIMIT, 0.0, 1.0)) > 0.5

    @pl.when(overflow)
    def _():
        m_ref[...] = jnp.full_like(m_ref, -jnp.inf)
        l_ref[...] = jnp.zeros_like(l_ref)
        acc_ref[...] = jnp.zeros_like(acc_ref)

        def block(t, carry):
            kb = key_block(t)
            vb = vT_ref[t]
            for c in range(ncol // wa):
                cols = slice(c * wa, (c + 1) * wa)
                kc = (lax.broadcasted_iota(jnp.int32, (tk, wa), 0) + t * tk) // CHUNK
                qc = ((lax.broadcasted_iota(jnp.int32, (tk, wa), 1) + c * wa) % tq + qi * tq) // CHUNK
                s = jnp.where(kc <= qc, scores(kb, c), -jnp.inf)
                m_old = m_ref[:, cols]
                m_new = jnp.maximum(m_old, jnp.max(s, axis=0, keepdims=True))
                alpha = jnp.exp2(m_old - m_new)
                p = jnp.exp2(s - m_new)
                l_ref[:, cols] = alpha * l_ref[:, cols] + jnp.sum(p, axis=0, keepdims=True)
                m_ref[:, cols] = m_new
                acc_ref[:, cols] = alpha * acc_ref[:, cols] + jnp.dot(
                    vb, p.astype(BF16), preferred_element_type=F32)
            return carry

        lax.fori_loop(0, n_blocks, block, 0)

    lam_p = lam_ref[...]
    lam = (jnp.exp(jnp.sum(lam_p[0:1] * lam_p[1:2], axis=1, keepdims=True))
           - jnp.exp(jnp.sum(lam_p[2:3] * lam_p[3:4], axis=1, keepdims=True)) + lam_init)
    acc = acc_ref[...]
    l = l_ref[...]
    out = acc[:, :tq] / l[:, :tq] - lam * (acc[:, tq:] / l[:, tq:])
    ms = jnp.mean(out * out, axis=0, keepdims=True)
    y = ((out * lax.rsqrt(ms + EPS)) * g_ref[...]) * (1.0 - lam_init)
    o_ref[...] = y.T.astype(o_ref.dtype)


def _attn(qvT, pn, lam_params, subln_col, lam_init):
    nblk = qvT.shape[0]
    S = pn.shape[0]
    tq, tk, blk = ATT_TQ, ATT_TK, ATT_BLK
    assert tk == blk and tq == 2 * tk
    k_col0 = AK_OFF // (2 * A_DH)
    return pl.pallas_call(
        functools.partial(_attn_kernel, lam_init=lam_init),
        grid=(A_HEADS, S // tq),
        in_specs=[
            pl.BlockSpec((tq // blk, 2 * A_DH, blk), lambda h, i: (i, h, 0)),
            pl.BlockSpec((S, 2 * A_DH), lambda h, i: (0, k_col0 + h)),
            pl.BlockSpec((nblk, A_DV, blk), lambda h, i: (0, A_HEADS + h, 0)),
            pl.BlockSpec((4, A_DH), lambda h, i: (0, 0)),
            pl.BlockSpec((A_DV, 1), lambda h, i: (0, 0)),
        ],
        out_specs=pl.BlockSpec((tq, A_DV), lambda h, i: (i, h)),
        out_shape=jax.ShapeDtypeStruct((S, A_WIDTH), BF16),
        scratch_shapes=[
            pltpu.VMEM((2 * A_DH, 2 * tq), BF16),
            pltpu.VMEM((tk, 2 * tq), F32),
            pltpu.VMEM((2, tk, 2 * tq), BF16),
            pltpu.VMEM((1, 2 * tq), F32),
            pltpu.VMEM((1, 2 * tq), F32),
            pltpu.VMEM((A_DV, 2 * tq), F32),
        ],
        compiler_params=_cparams(("parallel", "arbitrary")),
        name="diff_attn",
    )(qvT, pn, qvT, lam_params, subln_col)


def _out_proj_kernel(hm_ref, ha_ref, w_ref, x_ref, g_ref, x1_ref, h2_ref):
    r = jnp.dot(hm_ref[...], w_ref[0:M_WIDTH, :], preferred_element_type=F32)
    r = r + jnp.dot(ha_ref[...], w_ref[M_WIDTH:, :], preferred_element_type=F32)
    x1 = x_ref[...] + r
    x1_ref[...] = x1
    ms = jnp.mean(x1 * x1, axis=-1, keepdims=True)
    h2_ref[...] = ((x1 * lax.rsqrt(ms + EPS)) * g_ref[...]).astype(h2_ref.dtype)


def _out_proj(hm, ha, w_out, x2d, g):
    S = x2d.shape[0]
    tm = OUT_TM
    return pl.pallas_call(
        _out_proj_kernel,
        grid=(S // tm,),
        in_specs=[
            pl.BlockSpec((tm, M_WIDTH), lambda i: (i, 0)),
            pl.BlockSpec((tm, A_WIDTH), lambda i: (i, 0)),
            pl.BlockSpec((M_WIDTH + A_WIDTH, D_MODEL), lambda i: (0, 0)),
            pl.BlockSpec((tm, D_MODEL), lambda i: (i, 0)),
            pl.BlockSpec((1, D_MODEL), lambda i: (0, 0)),
        ],
        out_specs=[
            pl.BlockSpec((tm, D_MODEL), lambda i: (i, 0)),
            pl.BlockSpec((tm, D_MODEL), lambda i: (i, 0)),
        ],
        out_shape=[
            jax.ShapeDtypeStruct((S, D_MODEL), F32),
            jax.ShapeDtypeStruct((S, D_MODEL), BF16),
        ],
        compiler_params=_cparams(("parallel",)),
        name="out_proj",
    )(hm, ha, w_out, x2d, g)


def _swiglu_kernel(h2_ref, wg_ref, wu_ref, wd_ref, x1_ref, fg_ref, o_ref, acc_ref):
    j = pl.program_id(1)

    @pl.when(j == 0)
    def _():
        acc_ref[...] = jnp.zeros_like(acc_ref)

    h2 = h2_ref[...]
    g = jnp.dot(h2, wg_ref[...], preferred_element_type=F32)
    u = jnp.dot(h2, wu_ref[...], preferred_element_type=F32)
    a = ((g * jax.nn.sigmoid(g)) * u).astype(BF16)
    acc_ref[...] += jnp.dot(a, wd_ref[...], preferred_element_type=F32)

    @pl.when(j == pl.num_programs(1) - 1)
    def _():
        y = x1_ref[...] + acc_ref[...]
        ms = jnp.mean(y * y, axis=-1, keepdims=True)
        o_ref[...] = (y * lax.rsqrt(ms + EPS)) * fg_ref[...]


def _swiglu(h2, wg, wu, wd, x1, fg):
    S = h2.shape[0]
    tm, tf = FF_TM, FF_TF
    return pl.pallas_call(
        _swiglu_kernel,
        grid=(S // tm, D_FF // tf),
        in_specs=[
            pl.BlockSpec((tm, D_MODEL), lambda i, j: (i, 0)),
            pl.BlockSpec((D_MODEL, tf), lambda i, j: (0, j)),
            pl.BlockSpec((D_MODEL, tf), lambda i, j: (0, j)),
            pl.BlockSpec((tf, D_MODEL), lambda i, j: (j, 0)),
            pl.BlockSpec((tm, D_MODEL), lambda i, j: (i, 0)),
            pl.BlockSpec((1, D_MODEL), lambda i, j: (0, 0)),
        ],
        out_specs=pl.BlockSpec((tm, D_MODEL), lambda i, j: (i, 0)),
        out_shape=jax.ShapeDtypeStruct((S, D_MODEL), F32),
        scratch_shapes=[pltpu.VMEM((tm, D_MODEL), F32)],
        compiler_params=_cparams(("parallel", "arbitrary")),
        name="swiglu",
    )(h2, wg, wu, wd, x1, fg)


def _rope_tables(S):
    inv = ROPE_THETA ** (-jnp.arange(ROPE_HALF, dtype=F32) / ROPE_HALF)
    ang = jnp.arange(S).astype(F32)[:, None] * inv[None, :]
    cos = jnp.cos(ang)
    sin = jnp.sin(ang)
    cos_full = jnp.tile(cos, (1, LANES // ROPE_HALF))
    sin_signed = jnp.tile(jnp.concatenate([-sin, sin], axis=1), (1, LANES // A_DH))
    return cos_full, sin_signed, cos.T, sin.T


def kernel(x, norm1_g, w_in, conv_w, conv_b, b_igate, b_fgate, mnorm_g,
           lambda_q1, lambda_k1, lambda_q2, lambda_k2, subln_g, w_out,
           norm2_g, w_gate, w_up, w_down, final_g):
    B, S, D = x.shape
    assert (B, S, D) == (1, SEQ, D_MODEL) and norm1_g.shape[0] == 1
    l = 0
    x2d = x.reshape(S, D)

    w = w_in[l]
    o_mi = 2 * N_MQK + 2 * M_WIDTH
    o_aq = o_mi + 2 * M_HEADS
    o_ak = o_aq + A_WIDTH
    o_av = o_ak + A_WIDTH
    w_nat = jnp.concatenate([w[:, :o_mi], w[:, o_ak:o_av]], axis=1).astype(BF16)
    w_t = jnp.concatenate([w[:, o_aq:o_ak], w[:, o_av:]], axis=1).T.astype(BF16)
    w_g = jnp.pad(w[:, o_mi:o_aq], ((0, 0), (0, LANES - 2 * M_HEADS))).astype(BF16)
    b_g = jnp.pad(jnp.concatenate([b_igate[l], b_fgate[l]]), (0, LANES - 2 * M_HEADS)).reshape(1, LANES)

    cos_full, sin_signed, cosT, sinT = _rope_tables(S)

    gc, gr, pn, qvT = _front(x2d, norm1_g[l].reshape(1, D), w_g, b_g, w_nat, w_t,
                             cos_full, sin_signed, cosT, sinT)

    hm = _mlstm(pn, gc, gr, conv_w[l], conv_b[l].reshape(1, -1), mnorm_g[l].reshape(1, M_WIDTH))

    lam_init = 0.8 - 0.6 * math.exp(-0.3 * l)
    lam_params = jnp.stack([lambda_q1[l], lambda_k1[l], lambda_q2[l], lambda_k2[l]]).astype(F32)
    ha = _attn(qvT, pn, lam_params, subln_g[l].reshape(A_DV, 1), lam_init)

    x1, h2 = _out_proj(hm, ha, w_out[l].astype(BF16), x2d, norm2_g[l].reshape(1, D))
    out = _swiglu(h2, w_gate[l].astype(BF16), w_up[l].astype(BF16), w_down[l].astype(BF16),
                  x1, final_g.reshape(1, D))
    return out.reshape(B, S, D)
```

```python
import functools
import math

import jax
import jax.numpy as jnp
from jax import lax
from jax.experimental import pallas as pl
from jax.experimental.pallas import tpu as pltpu

F32 = jnp.float32
BF16 = jnp.bfloat16

D_MODEL = 2048
SEQ = 16384
CHUNK = 64
EPS = 1e-6
ROPE_THETA = 10000.0

M_HEADS = 4
M_DV = 256
M_DQK = 128
M_CONV = 4
GATE_CAP = 15.0
M_WIDTH = M_HEADS * M_DV
N_MQK = M_HEADS * M_DQK

A_HEADS = 8
A_DH = 64
A_DV = 128
A_WIDTH = A_HEADS * A_DV
ROPE_HALF = A_DH // 2

D_FF = 5632

LANES = 128
VMEM_LIMIT = 56 * 1024 * 1024

FRONT_TM = 512
FRONT_TN = 512
ATT_BLK = 512
ATT_TQ = 1024
ATT_TK = 512
LOG2E = math.log2(math.e)
ATT_L_LIMIT = 2.0 ** 60
ML_L = 128
ML_R = 512
OUT_TM = 512
FF_TM = 512
FF_TF = 512

N_NAT = 2 * N_MQK + 2 * M_WIDTH + A_WIDTH
AK_OFF = 2 * N_MQK + 2 * M_WIDTH
ONES_W = LANES


def _cparams(sem):
    return pltpu.CompilerParams(dimension_semantics=sem, vmem_limit_bytes=VMEM_LIMIT)


def _front_kernel(x_ref, g_ref, wg_ref, bg_ref, wn_ref, wt_ref, inv_row_ref, inv_col_ref, cw_ref, cb_ref,
                  gc_ref, gr_ref, pn_ref, qv_ref, h_ref, ybuf, tail_ref):
    tm = x_ref.shape[0]
    row0 = pl.program_id(0) * tm

    @pl.when(pl.program_id(0) == 0)
    def _():
        tail_ref[...] = jnp.zeros_like(tail_ref)

    x = x_ref[...]
    ms = jnp.mean(x * x, axis=-1, keepdims=True)
    h_ref[...] = ((x * lax.rsqrt(ms + EPS)) * g_ref[...]).astype(BF16)
    h = h_ref[...]

    pre = jnp.dot(h, wg_ref[...], preferred_element_type=F32) + bg_ref[...]
    capped = GATE_CAP * jnp.tanh(pre / GATE_CAP)
    logf = -(jnp.maximum(-capped, 0.0) + jnp.log1p(jnp.exp(-jnp.abs(capped))))
    lane = lax.broadcasted_iota(jnp.int32, pre.shape, 1)
    gates = jnp.where(lane < M_HEADS, capped, logf)
    gc_ref[...] = gates[:, :2 * M_HEADS]
    gates_t = gates.T[:2 * M_HEADS, :]
    for b in range(tm // LANES):
        gr_ref[b] = gates_t[:, b * LANES:(b + 1) * LANES]

    pos_c = (row0 + lax.broadcasted_iota(jnp.int32, (tm, LANES), 0)).astype(F32)
    ang = pos_c * inv_row_ref[...]
    ln = lax.broadcasted_iota(jnp.int32, (tm, LANES), 1)
    cos_rm = jnp.cos(ang)
    sin_rm = jnp.where((ln % A_DH) < ROPE_HALF, -jnp.sin(ang), jnp.sin(ang))
    pos_r = (row0 + lax.broadcasted_iota(jnp.int32, (ROPE_HALF, tm), 1)).astype(F32)
    ang_t = inv_col_ref[...] * pos_r
    cT = jnp.cos(ang_t)
    sT = jnp.sin(ang_t)

    tn = FRONT_TN
    reps = tn // LANES
    for j in range(N_NAT // tn):
        cols = slice(j * tn, (j + 1) * tn)
        r = jnp.dot(h, wn_ref[:, cols], preferred_element_type=F32)
        if (j + 1) * tn <= 2 * N_MQK:
            ybuf[0:8, :] = tail_ref[j]
            ybuf[8:8 + tm, :] = r
            tail_ref[j] = r[tm - 8:tm, :]
            y = cb_ref[:, cols]
            for tap in range(M_CONV - 1):
                off = 8 - (M_CONV - 1) + tap
                y = y + ybuf[off:off + tm, :] * cw_ref[tap:tap + 1, cols]
            y = y + r * cw_ref[M_CONV - 1:M_CONV, cols]
            r = y * jax.nn.sigmoid(y)
            if (j + 1) * tn <= N_MQK:
                r = r * (M_DQK ** -0.5)
        elif j * tn >= AK_OFF:
            c = jnp.concatenate([cos_rm] * reps, axis=1)
            s = jnp.concatenate([sin_rm] * reps, axis=1)
            first = (lax.broadcasted_iota(jnp.int32, r.shape, 1) % A_DH) < ROPE_HALF
            swapped = jnp.where(first, pltpu.roll(r, tn - ROPE_HALF, 1), pltpu.roll(r, ROPE_HALF, 1))
            r = r * c + swapped * s
        pn_ref[:, cols] = r.astype(pn_ref.dtype)

    rows = 256
    scale = A_DH ** -0.5 * LOG2E
    for rb in range(wt_ref.shape[0] // rows):
        r = lax.dot_general(wt_ref[rb * rows:(rb + 1) * rows, :], h,
                            (((1,), (1,)), ((), ())), preferred_element_type=F32)
        if rb * rows < A_WIDTH:
            parts = []
            for g in range(rows // A_DH):
                t1 = r[g * A_DH:g * A_DH + ROPE_HALF]
                t2 = r[g * A_DH + ROPE_HALF:(g + 1) * A_DH]
                parts.append((t1 * cT - t2 * sT) * scale)
                parts.append((t2 * cT + t1 * sT) * scale)
            r = jnp.concatenate(parts, axis=0)
        qv_ref[0, rb * rows:(rb + 1) * rows, :] = r.astype(qv_ref.dtype)


def _front(x2d, g, wg, bg, w_nat, w_t, inv_row, inv_col, conv_w, conv_b):
    S = x2d.shape[0]
    tm = FRONT_TM
    assert tm == ATT_BLK and FRONT_TN == N_MQK
    nfeat = w_t.shape[0]
    const = lambda i: (0, 0)
    resident = dict(pipeline_mode=pl.Buffered(1))
    return pl.pallas_call(
        _front_kernel,
        grid=(S // tm,),
        in_specs=[
            pl.BlockSpec((tm, D_MODEL), lambda i: (i, 0)),
            pl.BlockSpec((1, D_MODEL), const),
            pl.BlockSpec((D_MODEL, LANES), const),
            pl.BlockSpec((1, LANES), const),
            pl.BlockSpec((D_MODEL, N_NAT), const, **resident),
            pl.BlockSpec((nfeat, D_MODEL), const, **resident),
            pl.BlockSpec((1, LANES), const),
            pl.BlockSpec((ROPE_HALF, 1), const),
            pl.BlockSpec((M_CONV, 2 * N_MQK), const),
            pl.BlockSpec((1, 2 * N_MQK), const),
        ],
        out_specs=[
            pl.BlockSpec((tm, 2 * M_HEADS), lambda i: (i, 0)),
            pl.BlockSpec((tm // LANES, 2 * M_HEADS, LANES), lambda i: (i, 0, 0)),
            pl.BlockSpec((tm, N_NAT), lambda i: (i, 0)),
            pl.BlockSpec((1, nfeat, ATT_BLK), lambda i: (i, 0, 0)),
        ],
        out_shape=[
            jax.ShapeDtypeStruct((S, 2 * M_HEADS), F32),
            jax.ShapeDtypeStruct((S // LANES, 2 * M_HEADS, LANES), F32),
            jax.ShapeDtypeStruct((S, N_NAT), BF16),
            jax.ShapeDtypeStruct((S // ATT_BLK, nfeat, ATT_BLK), BF16),
        ],
        scratch_shapes=[
            pltpu.VMEM((tm, D_MODEL), BF16),
            pltpu.VMEM((tm + 8, FRONT_TN), F32),
            pltpu.VMEM((2 * N_MQK // FRONT_TN, 8, FRONT_TN), F32),
        ],
        compiler_params=_cparams(("arbitrary",)),
        name="front",
    )(x2d, g, wg, bg, w_nat, w_t, inv_row, inv_col, conv_w, conv_b)


def _split3(x):
    hi = x.astype(BF16)
    r1 = x - hi.astype(F32)
    mid = r1.astype(BF16)
    lo = (r1 - mid.astype(F32)).astype(BF16)
    return hi, mid, lo


def _mlstm_kernel(pm_ref, gc_ref, gr_ref, mg_ref, o_ref, c_ref, m_ref):
    R = pm_ref.shape[0]
    L = ML_L
    n_chunks = R // L
    NQK = 2 * N_MQK

    @pl.when(pl.program_id(0) == 0)
    def _():
        c_ref[...] = jnp.zeros_like(c_ref)
        m_ref[...] = jnp.zeros_like(m_ref)

    t_idx = lax.broadcasted_iota(jnp.int32, (L, L), 0)
    s_idx = lax.broadcasted_iota(jnp.int32, (L, L), 1)
    causal = s_idx <= t_idx
    tril = causal.astype(BF16)
    triu = (t_idx <= s_idx).astype(BF16)
    ones_ext = jnp.ones((L, ONES_W), BF16)

    for c in range(n_chunks):
        rows = slice(c * L, (c + 1) * L)
        gc = gc_ref[rows, :]
        gr = gr_ref[c]
        b_col_all = sum(jnp.dot(tril, p, preferred_element_type=F32) for p in _split3(gc))
        b_row_all = sum(jnp.dot(p, triu, preferred_element_type=F32) for p in _split3(gr))
        for hd in range(M_HEADS):
            qh = pm_ref[rows, hd * M_DQK:(hd + 1) * M_DQK]
            kh = pm_ref[rows, N_MQK + hd * M_DQK:N_MQK + (hd + 1) * M_DQK]
            vh = pm_ref[rows, NQK + hd * M_DV:NQK + (hd + 1) * M_DV]
            v_ext = jnp.concatenate([vh, ones_ext], axis=1)
            i_row = gr[hd:hd + 1, :]
            b_row = b_row_all[M_HEADS + hd:M_HEADS + hd + 1, :]
            b_col = b_col_all[:, M_HEADS + hd:M_HEADS + hd + 1]

            logd = jnp.where(causal, b_col - b_row + i_row, -jnp.inf)
            m_loc = jnp.max(logd, axis=1, keepdims=True)
            d_loc = jnp.exp(logd - m_loc)
            s = lax.dot_general(qh, kh, (((1,), (1,)), ((), ())), preferred_element_type=F32)
            p_ext = jnp.dot((s * d_loc).astype(BF16), v_ext, preferred_element_type=F32)

            b_last = b_row[:, L - 1:L]
            w_log = b_last - b_row + i_row
            a_loc = jnp.max(w_log, axis=1, keepdims=True)
            kw_t = (kh.astype(F32).T * jnp.exp(w_log - a_loc)).astype(BF16)
            kv_ext = jnp.dot(kw_t, v_ext, preferred_element_type=F32)

            m_prev = m_ref[hd:hd + 1, 0:1]
            c_prev = c_ref[hd]
            inter = b_col + m_prev
            m_t = jnp.maximum(inter, m_loc)
            qc = jnp.dot(qh, c_prev.astype(BF16), preferred_element_type=F32)
            nd = jnp.exp(inter - m_t) * qc + jnp.exp(m_loc - m_t) * p_ext
            num = nd[:, :M_DV]
            den = nd[:, M_DV:M_DV + 1]
            hh = num / jnp.maximum(jnp.abs(den), jnp.exp(-m_t))

            m_new = jnp.maximum(b_last + m_prev, a_loc)
            c_ref[hd] = jnp.exp(b_last + m_prev - m_new) * c_prev + jnp.exp(a_loc - m_new) * kv_ext
            m_ref[hd:hd + 1, :] = jnp.broadcast_to(m_new, (1, m_ref.shape[1]))

            cols = slice(hd * M_DV, (hd + 1) * M_DV)
            ms = jnp.mean(hh * hh, axis=-1, keepdims=True)
            hn = (hh * lax.rsqrt(ms + EPS)) * mg_ref[:, cols]
            mo = pm_ref[rows, NQK + M_WIDTH + hd * M_DV:NQK + M_WIDTH + (hd + 1) * M_DV].astype(F32)
            o_ref[rows, cols] = (hn * jax.nn.sigmoid(mo)).astype(o_ref.dtype)


def _mlstm(pn, gc, gr, mnorm_g):
    S = pn.shape[0]
    R = ML_R
    wm = 2 * N_MQK + 2 * M_WIDTH
    return pl.pallas_call(
        _mlstm_kernel,
        grid=(S // R,),
        in_specs=[
            pl.BlockSpec((R, wm), lambda i: (i, 0)),
            pl.BlockSpec((R, 2 * M_HEADS), lambda i: (i, 0)),
            pl.BlockSpec((R // ML_L, 2 * M_HEADS, ML_L), lambda i: (i, 0, 0)),
            pl.BlockSpec((1, M_WIDTH), lambda i: (0, 0)),
        ],
        out_specs=pl.BlockSpec((R, M_WIDTH), lambda i: (i, 0)),
        out_shape=jax.ShapeDtypeStruct((S, M_WIDTH), BF16),
        scratch_shapes=[
            pltpu.VMEM((M_HEADS, M_DQK, M_DV + ONES_W), F32),
            pltpu.VMEM((8, LANES), F32),
        ],
        compiler_params=_cparams(("arbitrary",)),
        name="mlstm",
    )(pn, gc, gr, mnorm_g)


def _attn_kernel(qT_ref, k_ref, vT_ref, lam_ref, g_ref, o_ref,
                 qz_ref, s0_ref, p_ref, m_ref, l_ref, acc_ref, *, lam_init):
    qi = pl.program_id(1)
    tq, tk, blk = ATT_TQ, ATT_TK, ATT_BLK
    n_blocks = (qi + 1) * (tq // tk)
    ncol = 2 * tq
    wa = 256

    frow = lax.broadcasted_iota(jnp.int32, (2 * A_DH, blk), 0)
    for b in range(tq // blk):
        qb = qT_ref[b]
        zero = jnp.zeros_like(qb)
        qz_ref[:, b * blk:(b + 1) * blk] = jnp.where(frow < A_DH, qb, zero)
        qz_ref[:, tq + b * blk:tq + (b + 1) * blk] = jnp.where(frow >= A_DH, qb, zero)

    def key_block(t):
        return k_ref[pl.ds(pl.multiple_of(t * tk, tk), tk), :]

    def diag_visible(c, d):
        kc = (lax.broadcasted_iota(jnp.int32, (tk, wa), 0) + d * tk) // CHUNK
        qc = ((lax.broadcasted_iota(jnp.int32, (tk, wa), 1) + c * wa) % tq) // CHUNK
        return kc <= qc

    def scores(kb, c):
        return jnp.dot(kb, qz_ref[:, c * wa:(c + 1) * wa], preferred_element_type=F32)

    all_chunks = tuple(range(ncol // wa))
    late_chunks = tuple(c for c in all_chunks if (c * wa) % tq >= tk)

    kb0 = key_block(0)
    for c in all_chunks:
        cols = slice(c * wa, (c + 1) * wa)
        vis = jnp.logical_or(diag_visible(c, 0), qi > 0)
        s = jnp.where(vis, scores(kb0, c), -jnp.inf)
        s0_ref[:, cols] = s
        m_ref[:, cols] = jnp.max(s, axis=0, keepdims=True)
    for c in all_chunks:
        cols = slice(c * wa, (c + 1) * wa)
        p = jnp.exp2(s0_ref[:, cols] - m_ref[:, cols])
        l_ref[:, cols] = jnp.sum(p, axis=0, keepdims=True)
        p_ref[0, :, cols] = p.astype(BF16)
    acc_ref[...] = jnp.zeros_like(acc_ref)

    def stage_ab(t, par, diag=None, chunks=all_chunks):
        kb = key_block(t)
        for c in chunks:
            cols = slice(c * wa, (c + 1) * wa)
            s = scores(kb, c)
            if diag is not None:
                s = jnp.where(diag_visible(c, diag), s, -jnp.inf)
            p = jnp.exp2(s - m_ref[:, cols])
            l_ref[:, cols] += jnp.sum(p, axis=0, keepdims=True)
            p_ref[par, :, cols] = p.astype(BF16)

    def stage_c(t, par, chunks=all_chunks):
        vb = vT_ref[t]
        for c in chunks:
            cols = slice(c * wa, (c + 1) * wa)
            acc_ref[:, cols] += jnp.dot(vb, p_ref[par, :, cols], preferred_element_type=F32)

    @pl.when(qi > 0)
    def _():
        def pair(t):
            stage_c(t, 0)
            stage_ab(t + 1, 1)
            stage_c(t + 1, 1)
            stage_ab(t + 2, 0)

        def two_pairs(v, carry):
            pair(4 * v)
            pair(4 * v + 2)
            return carry

        n_pairs = qi - 1
        lax.fori_loop(0, lax.shift_right_logical(n_pairs, 1), two_pairs, 0)

        @pl.when(jnp.bitwise_and(n_pairs, 1) == 1)
        def _():
            pair(2 * (n_pairs - 1))

        t = 2 * qi
        stage_c(t - 2, 0)
        stage_ab(t - 1, 1)
        stage_c(t - 1, 1)
        stage_ab(t, 0, diag=0)

    t = 2 * qi
    stage_c(t, 0)
    stage_ab(t + 1, 1, diag=1, chunks=late_chunks)
    stage_c(t + 1, 1, chunks=late_chunks)

    l_fast = l_ref[...]
    overflow = jnp.max(jnp.where(l_fast < ATT_L_LIMIT, 0.0, 1.0)) > 0.5

    @pl.when(overflow)
    def _():
        m_ref[...] = jnp.full_like(m_ref, -jnp.inf)
        l_ref[...] = jnp.zeros_like(l_ref)
        acc_ref[...] = jnp.zeros_like(acc_ref)

        def block(t, carry):
            kb = key_block(t)
            vb = vT_ref[t]
            for c in range(ncol // wa):
                cols = slice(c * wa, (c + 1) * wa)
                kc = (lax.broadcasted_iota(jnp.int32, (tk, wa), 0) + t * tk) // CHUNK
                qc = ((lax.broadcasted_iota(jnp.int32, (tk, wa), 1) + c * wa) % tq + qi * tq) // CHUNK
                s = jnp.where(kc <= qc, scores(kb, c), -jnp.inf)
                m_old = m_ref[:, cols]
                m_new = jnp.maximum(m_old, jnp.max(s, axis=0, keepdims=True))
                alpha = jnp.exp2(m_old - m_new)
                p = jnp.exp2(s - m_new)
                l_ref[:, cols] = alpha * l_ref[:, cols] + jnp.sum(p, axis=0, keepdims=True)
                m_ref[:, cols] = m_new
                acc_ref[:, cols] = alpha * acc_ref[:, cols] + jnp.dot(
                    vb, p.astype(BF16), preferred_element_type=F32)
            return carry

        lax.fori_loop(0, n_blocks, block, 0)

    lam_p = lam_ref[...]
    lam = (jnp.exp(jnp.sum(lam_p[0:1] * lam_p[1:2], axis=1, keepdims=True))
           - jnp.exp(jnp.sum(lam_p[2:3] * lam_p[3:4], axis=1, keepdims=True)) + lam_init)
    acc = acc_ref[...]
    l = l_ref[...]
    out = acc[:, :tq] / l[:, :tq] - lam * (acc[:, tq:] / l[:, tq:])
    ms = jnp.mean(out * out, axis=0, keepdims=True)
    y = ((out * lax.rsqrt(ms + EPS)) * g_ref[...]) * (1.0 - lam_init)
    o_ref[...] = y.T.astype(o_ref.dtype)


def _attn(qvT, pn, lam_params, subln_col, lam_init):
    nblk = qvT.shape[0]
    S = pn.shape[0]
    tq, tk, blk = ATT_TQ, ATT_TK, ATT_BLK
    assert tk == blk and tq == 2 * tk
    k_col0 = AK_OFF // (2 * A_DH)
    return pl.pallas_call(
        functools.partial(_attn_kernel, lam_init=lam_init),
        grid=(A_HEADS, S // tq),
        in_specs=[
            pl.BlockSpec((tq // blk, 2 * A_DH, blk), lambda h, i: (i, h, 0)),
            pl.BlockSpec((S, 2 * A_DH), lambda h, i: (0, k_col0 + h)),
            pl.BlockSpec((nblk, A_DV, blk), lambda h, i: (0, A_HEADS + h, 0)),
            pl.BlockSpec((4, A_DH), lambda h, i: (0, 0)),
            pl.BlockSpec((A_DV, 1), lambda h, i: (0, 0)),
        ],
        out_specs=pl.BlockSpec((tq, A_DV), lambda h, i: (i, h)),
        out_shape=jax.ShapeDtypeStruct((S, A_WIDTH), BF16),
        scratch_shapes=[
            pltpu.VMEM((2 * A_DH, 2 * tq), BF16),
            pltpu.VMEM((tk, 2 * tq), F32),
            pltpu.VMEM((2, tk, 2 * tq), BF16),
            pltpu.VMEM((1, 2 * tq), F32),
            pltpu.VMEM((1, 2 * tq), F32),
            pltpu.VMEM((A_DV, 2 * tq), F32),
        ],
        compiler_params=_cparams(("parallel", "arbitrary")),
        name="diff_attn",
    )(qvT, pn, qvT, lam_params, subln_col)


def _out_proj_kernel(hm_ref, ha_ref, w_ref, x_ref, g_ref, x1_ref, h2_ref):
    r = jnp.dot(hm_ref[...], w_ref[0:M_WIDTH, :], preferred_element_type=F32)
    r = r + jnp.dot(ha_ref[...], w_ref[M_WIDTH:, :], preferred_element_type=F32)
    x1 = x_ref[...] + r
    x1_ref[...] = x1
    ms = jnp.mean(x1 * x1, axis=-1, keepdims=True)
    h2_ref[...] = ((x1 * lax.rsqrt(ms + EPS)) * g_ref[...]).astype(h2_ref.dtype)


def _out_proj(hm, ha, w_out, x2d, g):
    S = x2d.shape[0]
    tm = OUT_TM
    return pl.pallas_call(
        _out_proj_kernel,
        grid=(S // tm,),
        in_specs=[
            pl.BlockSpec((tm, M_WIDTH), lambda i: (i, 0)),
            pl.BlockSpec((tm, A_WIDTH), lambda i: (i, 0)),
            pl.BlockSpec((M_WIDTH + A_WIDTH, D_MODEL), lambda i: (0, 0)),
            pl.BlockSpec((tm, D_MODEL), lambda i: (i, 0)),
            pl.BlockSpec((1, D_MODEL), lambda i: (0, 0)),
        ],
        out_specs=[
            pl.BlockSpec((tm, D_MODEL), lambda i: (i, 0)),
            pl.BlockSpec((tm, D_MODEL), lambda i: (i, 0)),
        ],
        out_shape=[
            jax.ShapeDtypeStruct((S, D_MODEL), F32),
            jax.ShapeDtypeStruct((S, D_MODEL), BF16),
        ],
        compiler_params=_cparams(("parallel",)),
        name="out_proj",
    )(hm, ha, w_out, x2d, g)


def _swiglu_kernel(h2_ref, wg_ref, wu_ref, wd_ref, x1_ref, fg_ref, o_ref, acc_ref):
    j = pl.program_id(1)

    @pl.when(j == 0)
    def _():
        acc_ref[...] = jnp.zeros_like(acc_ref)

    h2 = h2_ref[...]
    g = jnp.dot(h2, wg_ref[...], preferred_element_type=F32)
    u = jnp.dot(h2, wu_ref[...], preferred_element_type=F32)
    a = ((g * jax.nn.sigmoid(g)) * u).astype(BF16)
    acc_ref[...] += jnp.dot(a, wd_ref[...], preferred_element_type=F32)

    @pl.when(j == pl.num_programs(1) - 1)
    def _():
        y = x1_ref[...] + acc_ref[...]
        ms = jnp.mean(y * y, axis=-1, keepdims=True)
        o_ref[...] = (y * lax.rsqrt(ms + EPS)) * fg_ref[...]


def _swiglu(h2, wg, wu, wd, x1, fg):
    S = h2.shape[0]
    tm, tf = FF_TM, FF_TF
    return pl.pallas_call(
        _swiglu_kernel,
        grid=(S // tm, D_FF // tf),
        in_specs=[
            pl.BlockSpec((tm, D_MODEL), lambda i, j: (i, 0)),
            pl.BlockSpec((D_MODEL, tf), lambda i, j: (0, j)),
            pl.BlockSpec((D_MODEL, tf), lambda i, j: (0, j)),
            pl.BlockSpec((tf, D_MODEL), lambda i, j: (j, 0)),
            pl.BlockSpec((tm, D_MODEL), lambda i, j: (i, 0)),
            pl.BlockSpec((1, D_MODEL), lambda i, j: (0, 0)),
        ],
        out_specs=pl.BlockSpec((tm, D_MODEL), lambda i, j: (i, 0)),
        out_shape=jax.ShapeDtypeStruct((S, D_MODEL), F32),
        scratch_shapes=[pltpu.VMEM((tm, D_MODEL), F32)],
        compiler_params=_cparams(("parallel", "arbitrary")),
        name="swiglu",
    )(h2, wg, wu, wd, x1, fg)


def _rope_inv_freq():
    inv = ROPE_THETA ** (-jnp.arange(ROPE_HALF, dtype=F32) / ROPE_HALF)
    return jnp.tile(inv, LANES // ROPE_HALF).reshape(1, LANES), inv.reshape(ROPE_HALF, 1)


def kernel(x, norm1_g, w_in, conv_w, conv_b, b_igate, b_fgate, mnorm_g,
           lambda_q1, lambda_k1, lambda_q2, lambda_k2, subln_g, w_out,
           norm2_g, w_gate, w_up, w_down, final_g):
    B, S, D = x.shape
    assert (B, S, D) == (1, SEQ, D_MODEL) and norm1_g.shape[0] == 1
    l = 0
    x2d = x.reshape(S, D)

    w = w_in[l]
    o_mi = 2 * N_MQK + 2 * M_WIDTH
    o_aq = o_mi + 2 * M_HEADS
    o_ak = o_aq + A_WIDTH
    o_av = o_ak + A_WIDTH
    w_nat = jnp.concatenate([w[:, :o_mi], w[:, o_ak:o_av]], axis=1).astype(BF16)
    w_t = jnp.concatenate([w[:, o_aq:o_ak], w[:, o_av:]], axis=1).T.astype(BF16)
    w_g = jnp.pad(w[:, o_mi:o_aq], ((0, 0), (0, LANES - 2 * M_HEADS))).astype(BF16)
    b_g = jnp.pad(jnp.concatenate([b_igate[l], b_fgate[l]]), (0, LANES - 2 * M_HEADS)).reshape(1, LANES)

    inv_row, inv_col = _rope_inv_freq()

    gc, gr, pn, qvT = _front(x2d, norm1_g[l].reshape(1, D), w_g, b_g, w_nat, w_t,
                             inv_row, inv_col, conv_w[l], conv_b[l].reshape(1, -1))

    hm = _mlstm(pn, gc, gr, mnorm_g[l].reshape(1, M_WIDTH))

    lam_init = 0.8 - 0.6 * math.exp(-0.3 * l)
    lam_params = jnp.stack([lambda_q1[l], lambda_k1[l], lambda_q2[l], lambda_k2[l]]).astype(F32)
    ha = _attn(qvT, pn, lam_params, subln_g[l].reshape(A_DV, 1), lam_init)

    x1, h2 = _out_proj(hm, ha, w_out[l].astype(BF16), x2d, norm2_g[l].reshape(1, D))
    out = _swiglu(h2, w_gate[l].astype(BF16), w_up[l].astype(BF16), w_down[l].astype(BF16),
                  x1, final_g.reshape(1, D))
    return out.reshape(B, S, D)
```

```python
import functools
import math

import jax
import jax.numpy as jnp
from jax import lax
from jax.experimental import pallas as pl
from jax.experimental.pallas import tpu as pltpu

F32 = jnp.float32
BF16 = jnp.bfloat16

D_MODEL = 2048
SEQ = 16384
CHUNK = 64
EPS = 1e-6
ROPE_THETA = 10000.0

M_HEADS = 4
M_DV = 256
M_DQK = 128
M_CONV = 4
GATE_CAP = 15.0
M_WIDTH = M_HEADS * M_DV
N_MQK = M_HEADS * M_DQK

A_HEADS = 8
A_DH = 64
A_DV = 128
A_WIDTH = A_HEADS * A_DV
ROPE_HALF = A_DH // 2

D_FF = 5632

LANES = 128
VMEM_LIMIT = 56 * 1024 * 1024

FRONT_TM = 512
FRONT_TN = 512
ATT_BLK = 512
ATT_TQ = 2048
ATT_TK = 512
LOG2E = math.log2(math.e)
ATT_L_LIMIT = 2.0 ** 60
ML_L = 128
ML_R = 1024
OUT_TM = 512
FF_TM = 512
FF_TF = 512

N_NAT = 2 * N_MQK + 2 * M_WIDTH + A_WIDTH
AK_OFF = 2 * N_MQK + 2 * M_WIDTH
ONES_W = LANES


def _cparams(sem):
    return pltpu.CompilerParams(dimension_semantics=sem, vmem_limit_bytes=VMEM_LIMIT)


def _front_kernel(x_ref, g_ref, wg_ref, bg_ref, wn_ref, wt_ref, inv_row_ref, inv_col_ref, cw_ref, cb_ref,
                  gc_ref, gr_ref, pn_ref, qv_ref, h_ref, ybuf, tail_ref):
    tm = x_ref.shape[0]
    row0 = pl.program_id(0) * tm

    @pl.when(pl.program_id(0) == 0)
    def _():
        tail_ref[...] = jnp.zeros_like(tail_ref)

    x = x_ref[...]
    ms = jnp.mean(x * x, axis=-1, keepdims=True)
    h_ref[...] = ((x * lax.rsqrt(ms + EPS)) * g_ref[...]).astype(BF16)
    h = h_ref[...]

    pre = jnp.dot(h, wg_ref[...], preferred_element_type=F32) + bg_ref[...]
    capped = GATE_CAP * jnp.tanh(pre / GATE_CAP)
    logf = -(jnp.maximum(-capped, 0.0) + jnp.log1p(jnp.exp(-jnp.abs(capped))))
    lane = lax.broadcasted_iota(jnp.int32, pre.shape, 1)
    gates = jnp.where(lane < M_HEADS, capped, logf)
    gc_ref[...] = gates[:, :2 * M_HEADS]
    gates_t = gates.T[:2 * M_HEADS, :]
    for b in range(tm // LANES):
        gr_ref[b] = gates_t[:, b * LANES:(b + 1) * LANES]

    pos_c = (row0 + lax.broadcasted_iota(jnp.int32, (tm, LANES), 0)).astype(F32)
    ang = pos_c * inv_row_ref[...]
    ln = lax.broadcasted_iota(jnp.int32, (tm, LANES), 1)
    cos_rm = jnp.cos(ang)
    sin_rm = jnp.where((ln % A_DH) < ROPE_HALF, -jnp.sin(ang), jnp.sin(ang))
    pos_r = (row0 + lax.broadcasted_iota(jnp.int32, (ROPE_HALF, tm), 1)).astype(F32)
    ang_t = inv_col_ref[...] * pos_r
    cT = jnp.cos(ang_t)
    sT = jnp.sin(ang_t)

    tn = FRONT_TN
    reps = tn // LANES
    for j in range(N_NAT // tn):
        cols = slice(j * tn, (j + 1) * tn)
        r = jnp.dot(h, wn_ref[:, cols], preferred_element_type=F32)
        if (j + 1) * tn <= 2 * N_MQK:
            ybuf[0:8, :] = tail_ref[j]
            ybuf[8:8 + tm, :] = r
            tail_ref[j] = r[tm - 8:tm, :]
            y = cb_ref[:, cols]
            for tap in range(M_CONV - 1):
                off = 8 - (M_CONV - 1) + tap
                y = y + ybuf[off:off + tm, :] * cw_ref[tap:tap + 1, cols]
            y = y + r * cw_ref[M_CONV - 1:M_CONV, cols]
            r = y * jax.nn.sigmoid(y)
            if (j + 1) * tn <= N_MQK:
                r = r * (M_DQK ** -0.5)
        elif j * tn >= AK_OFF:
            c = jnp.concatenate([cos_rm] * reps, axis=1)
            s = jnp.concatenate([sin_rm] * reps, axis=1)
            first = (lax.broadcasted_iota(jnp.int32, r.shape, 1) % A_DH) < ROPE_HALF
            swapped = jnp.where(first, pltpu.roll(r, tn - ROPE_HALF, 1), pltpu.roll(r, ROPE_HALF, 1))
            r = r * c + swapped * s
        pn_ref[:, cols] = r.astype(pn_ref.dtype)

    rows = 256
    scale = A_DH ** -0.5 * LOG2E
    for rb in range(wt_ref.shape[0] // rows):
        r = lax.dot_general(wt_ref[rb * rows:(rb + 1) * rows, :], h,
                            (((1,), (1,)), ((), ())), preferred_element_type=F32)
        if rb * rows < A_WIDTH:
            parts = []
            for g in range(rows // A_DH):
                t1 = r[g * A_DH:g * A_DH + ROPE_HALF]
                t2 = r[g * A_DH + ROPE_HALF:(g + 1) * A_DH]
                parts.append((t1 * cT - t2 * sT) * scale)
                parts.append((t2 * cT + t1 * sT) * scale)
            r = jnp.concatenate(parts, axis=0)
        qv_ref[0, rb * rows:(rb + 1) * rows, :] = r.astype(qv_ref.dtype)


def _front(x2d, g, wg, bg, w_nat, w_t, inv_row, inv_col, conv_w, conv_b):
    S = x2d.shape[0]
    tm = FRONT_TM
    assert tm == ATT_BLK and FRONT_TN == N_MQK
    nfeat = w_t.shape[0]
    const = lambda i: (0, 0)
    resident = dict(pipeline_mode=pl.Buffered(1))
    return pl.pallas_call(
        _front_kernel,
        grid=(S // tm,),
        in_specs=[
            pl.BlockSpec((tm, D_MODEL), lambda i: (i, 0)),
            pl.BlockSpec((1, D_MODEL), const),
            pl.BlockSpec((D_MODEL, LANES), const),
            pl.BlockSpec((1, LANES), const),
            pl.BlockSpec((D_MODEL, N_NAT), const, **resident),
            pl.BlockSpec((nfeat, D_MODEL), const, **resident),
            pl.BlockSpec((1, LANES), const),
            pl.BlockSpec((ROPE_HALF, 1), const),
            pl.BlockSpec((M_CONV, 2 * N_MQK), const),
            pl.BlockSpec((1, 2 * N_MQK), const),
        ],
        out_specs=[
            pl.BlockSpec((tm, 2 * M_HEADS), lambda i: (i, 0)),
            pl.BlockSpec((tm // LANES, 2 * M_HEADS, LANES), lambda i: (i, 0, 0)),
            pl.BlockSpec((tm, N_NAT), lambda i: (i, 0)),
            pl.BlockSpec((1, nfeat, ATT_BLK), lambda i: (i, 0, 0)),
        ],
        out_shape=[
            jax.ShapeDtypeStruct((S, 2 * M_HEADS), F32),
            jax.ShapeDtypeStruct((S // LANES, 2 * M_HEADS, LANES), F32),
            jax.ShapeDtypeStruct((S, N_NAT), BF16),
            jax.ShapeDtypeStruct((S // ATT_BLK, nfeat, ATT_BLK), BF16),
        ],
        scratch_shapes=[
            pltpu.VMEM((tm, D_MODEL), BF16),
            pltpu.VMEM((tm + 8, FRONT_TN), F32),
            pltpu.VMEM((2 * N_MQK // FRONT_TN, 8, FRONT_TN), F32),
        ],
        compiler_params=_cparams(("arbitrary",)),
        name="front",
    )(x2d, g, wg, bg, w_nat, w_t, inv_row, inv_col, conv_w, conv_b)


def _split3(x):
    hi = x.astype(BF16)
    r1 = x - hi.astype(F32)
    mid = r1.astype(BF16)
    lo = (r1 - mid.astype(F32)).astype(BF16)
    return hi, mid, lo


def _mlstm_kernel(pm_ref, gc_ref, gr_ref, mg_ref, o_ref, c_ref, m_ref):
    R = pm_ref.shape[0]
    L = ML_L
    n_chunks = R // L
    NQK = 2 * N_MQK

    @pl.when(pl.program_id(0) == 0)
    def _():
        c_ref[...] = jnp.zeros_like(c_ref)
        m_ref[...] = jnp.zeros_like(m_ref)

    t_idx = lax.broadcasted_iota(jnp.int32, (L, L), 0)
    s_idx = lax.broadcasted_iota(jnp.int32, (L, L), 1)
    causal = s_idx <= t_idx
    tril = causal.astype(BF16)
    triu = (t_idx <= s_idx).astype(BF16)
    ones_ext = jnp.ones((L, ONES_W), BF16)

    for c in range(n_chunks):
        rows = slice(c * L, (c + 1) * L)
        gc = gc_ref[rows, :]
        gr = gr_ref[c]
        b_col_all = sum(jnp.dot(tril, p, preferred_element_type=F32) for p in _split3(gc))
        b_row_all = sum(jnp.dot(p, triu, preferred_element_type=F32) for p in _split3(gr))
        for hd in range(M_HEADS):
            qh = pm_ref[rows, hd * M_DQK:(hd + 1) * M_DQK]
            kh = pm_ref[rows, N_MQK + hd * M_DQK:N_MQK + (hd + 1) * M_DQK]
            vh = pm_ref[rows, NQK + hd * M_DV:NQK + (hd + 1) * M_DV]
            v_ext = jnp.concatenate([vh, ones_ext], axis=1)
            i_row = gr[hd:hd + 1, :]
            b_row = b_row_all[M_HEADS + hd:M_HEADS + hd + 1, :]
            b_col = b_col_all[:, M_HEADS + hd:M_HEADS + hd + 1]

            logd = jnp.where(causal, b_col - b_row + i_row, -jnp.inf)
            m_loc = jnp.max(logd, axis=1, keepdims=True)
            d_loc = jnp.exp(logd - m_loc)
            s = lax.dot_general(qh, kh, (((1,), (1,)), ((), ())), preferred_element_type=F32)
            p_ext = jnp.dot((s * d_loc).astype(BF16), v_ext, preferred_element_type=F32)

            b_last = b_row[:, L - 1:L]
            w_log = b_last - b_row + i_row
            a_loc = jnp.max(w_log, axis=1, keepdims=True)
            kw_t = (kh.astype(F32).T * jnp.exp(w_log - a_loc)).astype(BF16)
            kv_ext = jnp.dot(kw_t, v_ext, preferred_element_type=F32)

            m_prev = m_ref[hd:hd + 1, 0:1]
            c_prev = c_ref[hd]
            inter = b_col + m_prev
            m_t = jnp.maximum(inter, m_loc)
            qc = jnp.dot(qh, c_prev.astype(BF16), preferred_element_type=F32)
            nd = jnp.exp(inter - m_t) * qc + jnp.exp(m_loc - m_t) * p_ext
            num = nd[:, :M_DV]
            den = nd[:, M_DV:M_DV + 1]
            hh = num / jnp.maximum(jnp.abs(den), jnp.exp(-m_t))

            m_new = jnp.maximum(b_last + m_prev, a_loc)
            c_ref[hd] = jnp.exp(b_last + m_prev - m_new) * c_prev + jnp.exp(a_loc - m_new) * kv_ext
            m_ref[hd:hd + 1, :] = jnp.broadcast_to(m_new, (1, m_ref.shape[1]))

            cols = slice(hd * M_DV, (hd + 1) * M_DV)
            ms = jnp.mean(hh * hh, axis=-1, keepdims=True)
            hn = (hh * lax.rsqrt(ms + EPS)) * mg_ref[:, cols]
            mo = pm_ref[rows, NQK + M_WIDTH + hd * M_DV:NQK + M_WIDTH + (hd + 1) * M_DV].astype(F32)
            o_ref[rows, cols] = (hn * jax.nn.sigmoid(mo)).astype(o_ref.dtype)


def _mlstm(pn, gc, gr, mnorm_g):
    S = pn.shape[0]
    R = ML_R
    wm = 2 * N_MQK + 2 * M_WIDTH
    return pl.pallas_call(
        _mlstm_kernel,
        grid=(S // R,),
        in_specs=[
            pl.BlockSpec((R, wm), lambda i: (i, 0)),
            pl.BlockSpec((R, 2 * M_HEADS), lambda i: (i, 0)),
            pl.BlockSpec((R // ML_L, 2 * M_HEADS, ML_L), lambda i: (i, 0, 0)),
            pl.BlockSpec((1, M_WIDTH), lambda i: (0, 0)),
        ],
        out_specs=pl.BlockSpec((R, M_WIDTH), lambda i: (i, 0)),
        out_shape=jax.ShapeDtypeStruct((S, M_WIDTH), BF16),
        scratch_shapes=[
            pltpu.VMEM((M_HEADS, M_DQK, M_DV + ONES_W), F32),
            pltpu.VMEM((8, LANES), F32),
        ],
        compiler_params=_cparams(("arbitrary",)),
        name="mlstm",
    )(pn, gc, gr, mnorm_g)


def _attn_kernel(qT_ref, k_ref, vT_ref, lam_ref, g_ref, o_ref,
                 qz_ref, s0_ref, p_ref, m_ref, l_ref, acc_ref, *, lam_init):
    qi = pl.program_id(1)
    tq, tk, blk = ATT_TQ, ATT_TK, ATT_BLK
    n_blocks = (qi + 1) * (tq // tk)
    ncol = 2 * tq
    wa = 256

    frow = lax.broadcasted_iota(jnp.int32, (2 * A_DH, blk), 0)
    for b in range(tq // blk):
        qb = qT_ref[b]
        zero = jnp.zeros_like(qb)
        qz_ref[:, b * blk:(b + 1) * blk] = jnp.where(frow < A_DH, qb, zero)
        qz_ref[:, tq + b * blk:tq + (b + 1) * blk] = jnp.where(frow >= A_DH, qb, zero)

    def key_block(t):
        return k_ref[pl.ds(pl.multiple_of(t * tk, tk), tk), :]

    def diag_visible(c, d):
        kc = (lax.broadcasted_iota(jnp.int32, (tk, wa), 0) + d * tk) // CHUNK
        qc = ((lax.broadcasted_iota(jnp.int32, (tk, wa), 1) + c * wa) % tq) // CHUNK
        return kc <= qc

    def scores(kb, c):
        return jnp.dot(kb, qz_ref[:, c * wa:(c + 1) * wa], preferred_element_type=F32)

    all_chunks = tuple(range(ncol // wa))
    n_diag = tq // tk
    first_diag = qi * n_diag

    def chunks_from(d):
        return tuple(c for c in all_chunks if (c * wa) % tq >= d * tk)

    kb0 = key_block(0)
    for c in all_chunks:
        cols = slice(c * wa, (c + 1) * wa)
        vis = jnp.logical_or(diag_visible(c, 0), qi > 0)
        s = jnp.where(vis, scores(kb0, c), -jnp.inf)
        s0_ref[:, cols] = s
        m_ref[:, cols] = jnp.max(s, axis=0, keepdims=True)
    for c in all_chunks:
        cols = slice(c * wa, (c + 1) * wa)
        p = jnp.exp2(s0_ref[:, cols] - m_ref[:, cols])
        l_ref[:, cols] = jnp.sum(p, axis=0, keepdims=True)
        p_ref[0, :, cols] = p.astype(BF16)
    acc_ref[...] = jnp.zeros_like(acc_ref)

    def stage_ab(t, par, diag=None, chunks=all_chunks):
        kb = key_block(t)
        for c in chunks:
            cols = slice(c * wa, (c + 1) * wa)
            s = scores(kb, c)
            if diag is not None:
                s = jnp.where(diag_visible(c, diag), s, -jnp.inf)
            p = jnp.exp2(s - m_ref[:, cols])
            l_ref[:, cols] += jnp.sum(p, axis=0, keepdims=True)
            p_ref[par, :, cols] = p.astype(BF16)

    def stage_c(t, par, chunks=all_chunks):
        vb = vT_ref[t]
        for c in chunks:
            cols = slice(c * wa, (c + 1) * wa)
            acc_ref[:, cols] += jnp.dot(vb, p_ref[par, :, cols], preferred_element_type=F32)

    @pl.when(qi > 0)
    def _():
        def pair(t):
            stage_c(t, 0)
            stage_ab(t + 1, 1)
            stage_c(t + 1, 1)
            stage_ab(t + 2, 0)

        def two_pairs(v, carry):
            pair(4 * v)
            pair(4 * v + 2)
            return carry

        n_pairs = lax.shift_right_logical(first_diag, 1) - 1
        lax.fori_loop(0, lax.shift_right_logical(n_pairs, 1), two_pairs, 0)

        @pl.when(jnp.bitwise_and(n_pairs, 1) == 1)
        def _():
            pair(2 * (n_pairs - 1))

        stage_c(first_diag - 2, 0)
        stage_ab(first_diag - 1, 1)
        stage_c(first_diag - 1, 1)
        stage_ab(first_diag, 0, diag=0)

    for d in range(1, n_diag):
        stage_c(first_diag + d - 1, (d - 1) % 2, chunks=chunks_from(d - 1))
        stage_ab(first_diag + d, d % 2, diag=d, chunks=chunks_from(d))
    stage_c(first_diag + n_diag - 1, (n_diag - 1) % 2, chunks=chunks_from(n_diag - 1))

    l_fast = l_ref[...]
    overflow = jnp.max(jnp.where(l_fast < ATT_L_LIMIT, 0.0, 1.0)) > 0.5

    @pl.when(overflow)
    def _():
        m_ref[...] = jnp.full_like(m_ref, -jnp.inf)
        l_ref[...] = jnp.zeros_like(l_ref)
        acc_ref[...] = jnp.zeros_like(acc_ref)

        def block(t, carry):
            kb = key_block(t)
            vb = vT_ref[t]
            for c in range(ncol // wa):
                cols = slice(c * wa, (c + 1) * wa)
                kc = (lax.broadcasted_iota(jnp.int32, (tk, wa), 0) + t * tk) // CHUNK
                qc = ((lax.broadcasted_iota(jnp.int32, (tk, wa), 1) + c * wa) % tq + qi * tq) // CHUNK
                s = jnp.where(kc <= qc, scores(kb, c), -jnp.inf)
                m_old = m_ref[:, cols]
                m_new = jnp.maximum(m_old, jnp.max(s, axis=0, keepdims=True))
                alpha = jnp.exp2(m_old - m_new)
                p = jnp.exp2(s - m_new)
                l_ref[:, cols] = alpha * l_ref[:, cols] + jnp.sum(p, axis=0, keepdims=True)
                m_ref[:, cols] = m_new
                acc_ref[:, cols] = alpha * acc_ref[:, cols] + jnp.dot(
                    vb, p.astype(BF16), preferred_element_type=F32)
            return carry

        lax.fori_loop(0, n_blocks, block, 0)

    lam_p = lam_ref[...]
    lam = (jnp.exp(jnp.sum(lam_p[0:1] * lam_p[1:2], axis=1, keepdims=True))
           - jnp.exp(jnp.sum(lam_p[2:3] * lam_p[3:4], axis=1, keepdims=True)) + lam_init)
    acc = acc_ref[...]
    l = l_ref[...]
    out = acc[:, :tq] / l[:, :tq] - lam * (acc[:, tq:] / l[:, tq:])
    ms = jnp.mean(out * out, axis=0, keepdims=True)
    y = ((out * lax.rsqrt(ms + EPS)) * g_ref[...]) * (1.0 - lam_init)
    o_ref[...] = y.T.astype(o_ref.dtype)


def _attn(qvT, pn, lam_params, subln_col, lam_init):
    nblk = qvT.shape[0]
    S = pn.shape[0]
    tq, tk, blk = ATT_TQ, ATT_TK, ATT_BLK
    assert tk == blk and tq % (2 * tk) == 0
    k_col0 = AK_OFF // (2 * A_DH)
    return pl.pallas_call(
        functools.partial(_attn_kernel, lam_init=lam_init),
        grid=(A_HEADS, S // tq),
        in_specs=[
            pl.BlockSpec((tq // blk, 2 * A_DH, blk), lambda h, i: (i, h, 0)),
            pl.BlockSpec((S, 2 * A_DH), lambda h, i: (0, k_col0 + h)),
            pl.BlockSpec((nblk, A_DV, blk), lambda h, i: (0, A_HEADS + h, 0)),
            pl.BlockSpec((4, A_DH), lambda h, i: (0, 0)),
            pl.BlockSpec((A_DV, 1), lambda h, i: (0, 0)),
        ],
        out_specs=pl.BlockSpec((tq, A_DV), lambda h, i: (i, h)),
        out_shape=jax.ShapeDtypeStruct((S, A_WIDTH), BF16),
        scratch_shapes=[
            pltpu.VMEM((2 * A_DH, 2 * tq), BF16),
            pltpu.VMEM((tk, 2 * tq), F32),
            pltpu.VMEM((2, tk, 2 * tq), BF16),
            pltpu.VMEM((1, 2 * tq), F32),
            pltpu.VMEM((1, 2 * tq), F32),
            pltpu.VMEM((A_DV, 2 * tq), F32),
        ],
        compiler_params=_cparams(("parallel", "arbitrary")),
        name="diff_attn",
    )(qvT, pn, qvT, lam_params, subln_col)


def _out_proj_kernel(hm_ref, ha_ref, w_ref, x_ref, g_ref, x1_ref, h2_ref):
    r = jnp.dot(hm_ref[...], w_ref[0:M_WIDTH, :], preferred_element_type=F32)
    r = r + jnp.dot(ha_ref[...], w_ref[M_WIDTH:, :], preferred_element_type=F32)
    x1 = x_ref[...] + r
    x1_ref[...] = x1
    ms = jnp.mean(x1 * x1, axis=-1, keepdims=True)
    h2_ref[...] = ((x1 * lax.rsqrt(ms + EPS)) * g_ref[...]).astype(h2_ref.dtype)


def _out_proj(hm, ha, w_out, x2d, g):
    S = x2d.shape[0]
    tm = OUT_TM
    return pl.pallas_call(
        _out_proj_kernel,
        grid=(S // tm,),
        in_specs=[
            pl.BlockSpec((tm, M_WIDTH), lambda i: (i, 0)),
            pl.BlockSpec((tm, A_WIDTH), lambda i: (i, 0)),
            pl.BlockSpec((M_WIDTH + A_WIDTH, D_MODEL), lambda i: (0, 0)),
            pl.BlockSpec((tm, D_MODEL), lambda i: (i, 0)),
            pl.BlockSpec((1, D_MODEL), lambda i: (0, 0)),
        ],
        out_specs=[
            pl.BlockSpec((tm, D_MODEL), lambda i: (i, 0)),
            pl.BlockSpec((tm, D_MODEL), lambda i: (i, 0)),
        ],
        out_shape=[
            jax.ShapeDtypeStruct((S, D_MODEL), F32),
            jax.ShapeDtypeStruct((S, D_MODEL), BF16),
        ],
        compiler_params=_cparams(("parallel",)),
        name="out_proj",
    )(hm, ha, w_out, x2d, g)


def _swiglu_kernel(h2_ref, wg_ref, wu_ref, wd_ref, x1_ref, fg_ref, o_ref, acc_ref):
    j = pl.program_id(1)

    @pl.when(j == 0)
    def _():
        acc_ref[...] = jnp.zeros_like(acc_ref)

    h2 = h2_ref[...]
    g = jnp.dot(h2, wg_ref[...], preferred_element_type=F32)
    u = jnp.dot(h2, wu_ref[...], preferred_element_type=F32)
    a = ((g * jax.nn.sigmoid(g)) * u).astype(BF16)
    acc_ref[...] += jnp.dot(a, wd_ref[...], preferred_element_type=F32)

    @pl.when(j == pl.num_programs(1) - 1)
    def _():
        y = x1_ref[...] + acc_ref[...]
        ms = jnp.mean(y * y, axis=-1, keepdims=True)
        o_ref[...] = (y * lax.rsqrt(ms + EPS)) * fg_ref[...]


def _swiglu(h2, wg, wu, wd, x1, fg):
    S = h2.shape[0]
    tm, tf = FF_TM, FF_TF
    return pl.pallas_call(
        _swiglu_kernel,
        grid=(S // tm, D_FF // tf),
        in_specs=[
            pl.BlockSpec((tm, D_MODEL), lambda i, j: (i, 0)),
            pl.BlockSpec((D_MODEL, tf), lambda i, j: (0, j)),
            pl.BlockSpec((D_MODEL, tf), lambda i, j: (0, j)),
            pl.BlockSpec((tf, D_MODEL), lambda i, j: (j, 0)),
            pl.BlockSpec((tm, D_MODEL), lambda i, j: (i, 0)),
            pl.BlockSpec((1, D_MODEL), lambda i, j: (0, 0)),
        ],
        out_specs=pl.BlockSpec((tm, D_MODEL), lambda i, j: (i, 0)),
        out_shape=jax.ShapeDtypeStruct((S, D_MODEL), F32),
        scratch_shapes=[pltpu.VMEM((tm, D_MODEL), F32)],
        compiler_params=_cparams(("parallel", "arbitrary")),
        name="swiglu",
    )(h2, wg, wu, wd, x1, fg)


def _rope_inv_freq():
    inv = ROPE_THETA ** (-jnp.arange(ROPE_HALF, dtype=F32) / ROPE_HALF)
    return jnp.tile(inv, LANES // ROPE_HALF).reshape(1, LANES), inv.reshape(ROPE_HALF, 1)


def kernel(x, norm1_g, w_in, conv_w, conv_b, b_igate, b_fgate, mnorm_g,
           lambda_q1, lambda_k1, lambda_q2, lambda_k2, subln_g, w_out,
           norm2_g, w_gate, w_up, w_down, final_g):
    B, S, D = x.shape
    assert (B, S, D) == (1, SEQ, D_MODEL) and norm1_g.shape[0] == 1
    l = 0
    x2d = x.reshape(S, D)

    w = w_in[l]
    o_mi = 2 * N_MQK + 2 * M_WIDTH
    o_aq = o_mi + 2 * M_HEADS
    o_ak = o_aq + A_WIDTH
    o_av = o_ak + A_WIDTH
    w_nat = jnp.concatenate([w[:, :o_mi], w[:, o_ak:o_av]], axis=1).astype(BF16)
    w_t = jnp.concatenate([w[:, o_aq:o_ak], w[:, o_av:]], axis=1).T.astype(BF16)
    w_g = jnp.pad(w[:, o_mi:o_aq], ((0, 0), (0, LANES - 2 * M_HEADS))).astype(BF16)
    b_g = jnp.pad(jnp.concatenate([b_igate[l], b_fgate[l]]), (0, LANES - 2 * M_HEADS)).reshape(1, LANES)

    inv_row, inv_col = _rope_inv_freq()

    gc, gr, pn, qvT = _front(x2d, norm1_g[l].reshape(1, D), w_g, b_g, w_nat, w_t,
                             inv_row, inv_col, conv_w[l], conv_b[l].reshape(1, -1))

    hm = _mlstm(pn, gc, gr, mnorm_g[l].reshape(1, M_WIDTH))

    lam_init = 0.8 - 0.6 * math.exp(-0.3 * l)
    lam_params = jnp.stack([lambda_q1[l], lambda_k1[l], lambda_q2[l], lambda_k2[l]]).astype(F32)
    ha = _attn(qvT, pn, lam_params, subln_g[l].reshape(A_DV, 1), lam_init)

    x1, h2 = _out_proj(hm, ha, w_out[l].astype(BF16), x2d, norm2_g[l].reshape(1, D))
    out = _swiglu(h2, w_gate[l].astype(BF16), w_up[l].astype(BF16), w_down[l].astype(BF16),
                  x1, final_g.reshape(1, D))
    return out.reshape(B, S, D)
```

```python
import functools
import math

import jax
import jax.numpy as jnp
from jax import lax
from jax.experimental import pallas as pl
from jax.experimental.pallas import tpu as pltpu

F32 = jnp.float32
BF16 = jnp.bfloat16

D_MODEL = 2048
SEQ = 16384
CHUNK = 64
EPS = 1e-6
ROPE_THETA = 10000.0

M_HEADS = 4
M_DV = 256
M_DQK = 128
M_CONV = 4
GATE_CAP = 15.0
M_WIDTH = M_HEADS * M_DV
N_MQK = M_HEADS * M_DQK

A_HEADS = 8
A_DH = 64
A_DV = 128
A_WIDTH = A_HEADS * A_DV
ROPE_HALF = A_DH // 2

D_FF = 5632

LANES = 128
VMEM_LIMIT = 56 * 1024 * 1024

FRONT_TM = 512
FRONT_TN = 512
ATT_BLK = 512
ATT_TQ = 2048
ATT_TK = 512
LOG2E = math.log2(math.e)
ATT_L_LIMIT = 2.0 ** 60
ML_L = 128
ML_R = 1024
OUT_TM = 512
FF_TM = 512
FF_TF = 512

N_NAT = 2 * N_MQK + 2 * M_WIDTH + A_WIDTH
AK_OFF = 2 * N_MQK + 2 * M_WIDTH
ONES_W = LANES


def _cparams(sem):
    return pltpu.CompilerParams(dimension_semantics=sem, vmem_limit_bytes=VMEM_LIMIT)


def _front_kernel(x_ref, g_ref, wg_ref, bg_ref, wn_ref, wt_ref, inv_row_ref, inv_col_ref, cw_ref, cb_ref,
                  gc_ref, gr_ref, pn_ref, qv_ref, h_ref, ybuf, tail_ref):
    tm = x_ref.shape[0]
    row0 = pl.program_id(0) * tm

    @pl.when(pl.program_id(0) == 0)
    def _():
        tail_ref[...] = jnp.zeros_like(tail_ref)

    x = x_ref[...]
    ms = jnp.mean(x * x, axis=-1, keepdims=True)
    h_ref[...] = ((x * lax.rsqrt(ms + EPS)) * g_ref[...]).astype(BF16)
    h = h_ref[...]

    pre = jnp.dot(h, wg_ref[...], preferred_element_type=F32) + bg_ref[...]
    capped = GATE_CAP * jnp.tanh(pre / GATE_CAP)
    logf = -(jnp.maximum(-capped, 0.0) + jnp.log1p(jnp.exp(-jnp.abs(capped))))
    lane = lax.broadcasted_iota(jnp.int32, pre.shape, 1)
    gates = jnp.where(lane < M_HEADS, capped, logf)
    gc_ref[...] = gates[:, :2 * M_HEADS]
    gates_t = gates.T[:2 * M_HEADS, :]
    for b in range(tm // LANES):
        gr_ref[b] = gates_t[:, b * LANES:(b + 1) * LANES]

    pos_c = (row0 + lax.broadcasted_iota(jnp.int32, (tm, LANES), 0)).astype(F32)
    ang = pos_c * inv_row_ref[...]
    ln = lax.broadcasted_iota(jnp.int32, (tm, LANES), 1)
    cos_rm = jnp.cos(ang)
    sin_rm = jnp.where((ln % A_DH) < ROPE_HALF, -jnp.sin(ang), jnp.sin(ang))
    pos_r = (row0 + lax.broadcasted_iota(jnp.int32, (ROPE_HALF, tm), 1)).astype(F32)
    ang_t = inv_col_ref[...] * pos_r
    cT = jnp.cos(ang_t)
    sT = jnp.sin(ang_t)

    tn = FRONT_TN
    reps = tn // LANES
    for j in range(N_NAT // tn):
        cols = slice(j * tn, (j + 1) * tn)
        r = jnp.dot(h, wn_ref[:, cols], preferred_element_type=F32)
        if (j + 1) * tn <= 2 * N_MQK:
            ybuf[0:8, :] = tail_ref[j]
            ybuf[8:8 + tm, :] = r
            tail_ref[j] = r[tm - 8:tm, :]
            y = cb_ref[:, cols]
            for tap in range(M_CONV - 1):
                off = 8 - (M_CONV - 1) + tap
                y = y + ybuf[off:off + tm, :] * cw_ref[tap:tap + 1, cols]
            y = y + r * cw_ref[M_CONV - 1:M_CONV, cols]
            r = y * jax.nn.sigmoid(y)
            if (j + 1) * tn <= N_MQK:
                r = r * (M_DQK ** -0.5)
        elif j * tn >= AK_OFF:
            c = jnp.concatenate([cos_rm] * reps, axis=1)
            s = jnp.concatenate([sin_rm] * reps, axis=1)
            first = (lax.broadcasted_iota(jnp.int32, r.shape, 1) % A_DH) < ROPE_HALF
            swapped = jnp.where(first, pltpu.roll(r, tn - ROPE_HALF, 1), pltpu.roll(r, ROPE_HALF, 1))
            r = r * c + swapped * s
        pn_ref[:, cols] = r.astype(pn_ref.dtype)

    rows = 256
    scale = A_DH ** -0.5 * LOG2E
    for rb in range(wt_ref.shape[0] // rows):
        r = lax.dot_general(wt_ref[rb * rows:(rb + 1) * rows, :], h,
                            (((1,), (1,)), ((), ())), preferred_element_type=F32)
        if rb * rows < A_WIDTH:
            parts = []
            for g in range(rows // A_DH):
                t1 = r[g * A_DH:g * A_DH + ROPE_HALF]
                t2 = r[g * A_DH + ROPE_HALF:(g + 1) * A_DH]
                parts.append((t1 * cT - t2 * sT) * scale)
                parts.append((t2 * cT + t1 * sT) * scale)
            r = jnp.concatenate(parts, axis=0)
        qv_ref[0, rb * rows:(rb + 1) * rows, :] = r.astype(qv_ref.dtype)


def _front(x2d, g, wg, bg, w_nat, w_t, inv_row, inv_col, conv_w, conv_b):
    S = x2d.shape[0]
    tm = FRONT_TM
    assert tm == ATT_BLK and FRONT_TN == N_MQK
    nfeat = w_t.shape[0]
    const = lambda i: (0, 0)
    resident = dict(pipeline_mode=pl.Buffered(1))
    return pl.pallas_call(
        _front_kernel,
        grid=(S // tm,),
        in_specs=[
            pl.BlockSpec((tm, D_MODEL), lambda i: (i, 0)),
            pl.BlockSpec((1, D_MODEL), const),
            pl.BlockSpec((D_MODEL, LANES), const),
            pl.BlockSpec((1, LANES), const),
            pl.BlockSpec((D_MODEL, N_NAT), const, **resident),
            pl.BlockSpec((nfeat, D_MODEL), const, **resident),
            pl.BlockSpec((1, LANES), const),
            pl.BlockSpec((ROPE_HALF, 1), const),
            pl.BlockSpec((M_CONV, 2 * N_MQK), const),
            pl.BlockSpec((1, 2 * N_MQK), const),
        ],
        out_specs=[
            pl.BlockSpec((tm, 2 * M_HEADS), lambda i: (i, 0)),
            pl.BlockSpec((tm // LANES, 2 * M_HEADS, LANES), lambda i: (i, 0, 0)),
            pl.BlockSpec((tm, N_NAT), lambda i: (i, 0)),
            pl.BlockSpec((1, nfeat, ATT_BLK), lambda i: (i, 0, 0)),
        ],
        out_shape=[
            jax.ShapeDtypeStruct((S, 2 * M_HEADS), F32),
            jax.ShapeDtypeStruct((S // LANES, 2 * M_HEADS, LANES), F32),
            jax.ShapeDtypeStruct((S, N_NAT), BF16),
            jax.ShapeDtypeStruct((S // ATT_BLK, nfeat, ATT_BLK), BF16),
        ],
        scratch_shapes=[
            pltpu.VMEM((tm, D_MODEL), BF16),
            pltpu.VMEM((tm + 8, FRONT_TN), F32),
            pltpu.VMEM((2 * N_MQK // FRONT_TN, 8, FRONT_TN), F32),
        ],
        compiler_params=_cparams(("arbitrary",)),
        name="front",
    )(x2d, g, wg, bg, w_nat, w_t, inv_row, inv_col, conv_w, conv_b)


def _split3(x):
    hi = x.astype(BF16)
    r1 = x - hi.astype(F32)
    mid = r1.astype(BF16)
    lo = (r1 - mid.astype(F32)).astype(BF16)
    return hi, mid, lo


def _mlstm_kernel(pm_ref, gc_ref, gr_ref, mg_ref, o_ref, c_ref, m_ref):
    R = pm_ref.shape[0]
    L = ML_L
    n_chunks = R // L
    NQK = 2 * N_MQK

    @pl.when(pl.program_id(0) == 0)
    def _():
        c_ref[...] = jnp.zeros_like(c_ref)
        m_ref[...] = jnp.zeros_like(m_ref)

    t_idx = lax.broadcasted_iota(jnp.int32, (L, L), 0)
    s_idx = lax.broadcasted_iota(jnp.int32, (L, L), 1)
    causal = s_idx <= t_idx
    tril = causal.astype(BF16)
    triu = (t_idx <= s_idx).astype(BF16)
    ones_ext = jnp.ones((L, ONES_W), BF16)

    for c in range(n_chunks):
        rows = slice(c * L, (c + 1) * L)
        gc = gc_ref[rows, :]
        gr = gr_ref[c]
        b_col_all = sum(jnp.dot(tril, p, preferred_element_type=F32) for p in _split3(gc))
        b_row_all = sum(jnp.dot(p, triu, preferred_element_type=F32) for p in _split3(gr))
        for hd in range(M_HEADS):
            qh = pm_ref[rows, hd * M_DQK:(hd + 1) * M_DQK]
            kh = pm_ref[rows, N_MQK + hd * M_DQK:N_MQK + (hd + 1) * M_DQK]
            vh = pm_ref[rows, NQK + hd * M_DV:NQK + (hd + 1) * M_DV]
            v_ext = jnp.concatenate([vh, ones_ext], axis=1)
            i_row = gr[hd:hd + 1, :]
            b_row = b_row_all[M_HEADS + hd:M_HEADS + hd + 1, :]
            b_col = b_col_all[:, M_HEADS + hd:M_HEADS + hd + 1]

            logd = jnp.where(causal, b_col - b_row + i_row, -jnp.inf)
            m_loc = jnp.max(logd, axis=1, keepdims=True)
            d_loc = jnp.exp(logd - m_loc)
            s = lax.dot_general(qh, kh, (((1,), (1,)), ((), ())), preferred_element_type=F32)
            p_ext = jnp.dot((s * d_loc).astype(BF16), v_ext, preferred_element_type=F32)

            b_last = b_row[:, L - 1:L]
            w_log = b_last - b_row + i_row
            a_loc = jnp.max(w_log, axis=1, keepdims=True)
            kw_t = (kh.astype(F32).T * jnp.exp(w_log - a_loc)).astype(BF16)
            kv_ext = jnp.dot(kw_t, v_ext, preferred_element_type=F32)

            m_prev = m_ref[hd:hd + 1, 0:1]
            c_prev = c_ref[hd]
            inter = b_col + m_prev
            m_t = jnp.maximum(inter, m_loc)
            qc = jnp.dot(qh, c_prev.astype(BF16), preferred_element_type=F32)
            nd = jnp.exp(inter - m_t) * qc + jnp.exp(m_loc - m_t) * p_ext
            num = nd[:, :M_DV]
            den = nd[:, M_DV:M_DV + 1]
            hh = num / jnp.maximum(jnp.abs(den), jnp.exp(-m_t))

            m_new = jnp.maximum(b_last + m_prev, a_loc)
            c_ref[hd] = jnp.exp(b_last + m_prev - m_new) * c_prev + jnp.exp(a_loc - m_new) * kv_ext
            m_ref[hd:hd + 1, :] = jnp.broadcast_to(m_new, (1, m_ref.shape[1]))

            cols = slice(hd * M_DV, (hd + 1) * M_DV)
            ms = jnp.mean(hh * hh, axis=-1, keepdims=True)
            hn = (hh * lax.rsqrt(ms + EPS)) * mg_ref[:, cols]
            mo = pm_ref[rows, NQK + M_WIDTH + hd * M_DV:NQK + M_WIDTH + (hd + 1) * M_DV].astype(F32)
            o_ref[rows, cols] = (hn * jax.nn.sigmoid(mo)).astype(o_ref.dtype)


def _mlstm(pn, gc, gr, mnorm_g):
    S = pn.shape[0]
    R = ML_R
    wm = 2 * N_MQK + 2 * M_WIDTH
    return pl.pallas_call(
        _mlstm_kernel,
        grid=(S // R,),
        in_specs=[
            pl.BlockSpec((R, wm), lambda i: (i, 0)),
            pl.BlockSpec((R, 2 * M_HEADS), lambda i: (i, 0)),
            pl.BlockSpec((R // ML_L, 2 * M_HEADS, ML_L), lambda i: (i, 0, 0)),
            pl.BlockSpec((1, M_WIDTH), lambda i: (0, 0)),
        ],
        out_specs=pl.BlockSpec((R, M_WIDTH), lambda i: (i, 0)),
        out_shape=jax.ShapeDtypeStruct((S, M_WIDTH), BF16),
        scratch_shapes=[
            pltpu.VMEM((M_HEADS, M_DQK, M_DV + ONES_W), F32),
            pltpu.VMEM((8, LANES), F32),
        ],
        compiler_params=_cparams(("arbitrary",)),
        name="mlstm",
    )(pn, gc, gr, mnorm_g)


def _attn_kernel(qT_ref, k_ref, vT_ref, lam_ref, g_ref, o_ref,
                 qz_ref, p_ref, m_ref, l_ref, acc_ref, *, lam_init):
    qi = pl.program_id(1)
    tq, tk, blk = ATT_TQ, ATT_TK, ATT_BLK
    n_blocks = (qi + 1) * (tq // tk)
    ncol = 2 * tq
    wa = 256

    frow = lax.broadcasted_iota(jnp.int32, (2 * A_DH, blk), 0)
    for b in range(tq // blk):
        qb = qT_ref[b]
        zero = jnp.zeros_like(qb)
        qz_ref[:, b * blk:(b + 1) * blk] = jnp.where(frow < A_DH, qb, zero)
        qz_ref[:, tq + b * blk:tq + (b + 1) * blk] = jnp.where(frow >= A_DH, qb, zero)

    def key_block(t):
        return k_ref[pl.ds(pl.multiple_of(t * tk, tk), tk), :]

    def diag_visible(c, d):
        kc = (lax.broadcasted_iota(jnp.int32, (tk, wa), 0) + d * tk) // CHUNK
        qc = ((lax.broadcasted_iota(jnp.int32, (tk, wa), 1) + c * wa) % tq) // CHUNK
        return kc <= qc

    def scores(kb, c):
        return jnp.dot(kb, qz_ref[:, c * wa:(c + 1) * wa], preferred_element_type=F32)

    all_chunks = tuple(range(ncol // wa))
    n_diag = tq // tk
    first_diag = qi * n_diag

    def chunks_from(d):
        return tuple(c for c in all_chunks if (c * wa) % tq >= d * tk)

    k_first = k_ref[0:CHUNK, :]
    for c in all_chunks:
        cols = slice(c * wa, (c + 1) * wa)
        m_ref[:, cols] = jnp.max(scores(k_first, c), axis=0, keepdims=True)
    l_ref[...] = jnp.zeros_like(l_ref)
    acc_ref[...] = jnp.zeros_like(acc_ref)

    kb0 = key_block(0)
    for c in all_chunks:
        cols = slice(c * wa, (c + 1) * wa)
        vis = jnp.logical_or(diag_visible(c, 0), qi > 0)
        p = jnp.exp2(jnp.where(vis, scores(kb0, c), -jnp.inf) - m_ref[:, cols])
        l_ref[:, cols] += jnp.sum(p, axis=0, keepdims=True)
        p_ref[0, :, cols] = p.astype(BF16)

    def stage_ab(t, par, diag=None, chunks=all_chunks):
        kb = key_block(t)
        for c in chunks:
            cols = slice(c * wa, (c + 1) * wa)
            s = scores(kb, c)
            if diag is not None:
                s = jnp.where(diag_visible(c, diag), s, -jnp.inf)
            p = jnp.exp2(s - m_ref[:, cols])
            l_ref[:, cols] += jnp.sum(p, axis=0, keepdims=True)
            p_ref[par, :, cols] = p.astype(BF16)

    def stage_c(t, par, chunks=all_chunks):
        vb = vT_ref[t]
        for c in chunks:
            cols = slice(c * wa, (c + 1) * wa)
            acc_ref[:, cols] += jnp.dot(vb, p_ref[par, :, cols], preferred_element_type=F32)

    @pl.when(qi > 0)
    def _():
        def pair(t):
            stage_c(t, 0)
            stage_ab(t + 1, 1)
            stage_c(t + 1, 1)
            stage_ab(t + 2, 0)

        def two_pairs(v, carry):
            pair(4 * v)
            pair(4 * v + 2)
            return carry

        n_pairs = lax.shift_right_logical(first_diag, 1) - 1
        lax.fori_loop(0, lax.shift_right_logical(n_pairs, 1), two_pairs, 0)

        @pl.when(jnp.bitwise_and(n_pairs, 1) == 1)
        def _():
            pair(2 * (n_pairs - 1))

        stage_c(first_diag - 2, 0)
        stage_ab(first_diag - 1, 1)
        stage_c(first_diag - 1, 1)
        stage_ab(first_diag, 0, diag=0)

    for d in range(1, n_diag):
        stage_c(first_diag + d - 1, (d - 1) % 2, chunks=chunks_from(d - 1))
        stage_ab(first_diag + d, d % 2, diag=d, chunks=chunks_from(d))
    stage_c(first_diag + n_diag - 1, (n_diag - 1) % 2, chunks=chunks_from(n_diag - 1))

    l_fast = l_ref[...]
    overflow = jnp.max(jnp.where(l_fast < ATT_L_LIMIT, 0.0, 1.0)) > 0.5

    @pl.when(overflow)
    def _():
        m_ref[...] = jnp.full_like(m_ref, -jnp.inf)
        l_ref[...] = jnp.zeros_like(l_ref)
        acc_ref[...] = jnp.zeros_like(acc_ref)

        def block(t, carry):
            kb = key_block(t)
            vb = vT_ref[t]
            for c in range(ncol // wa):
                cols = slice(c * wa, (c + 1) * wa)
                kc = (lax.broadcasted_iota(jnp.int32, (tk, wa), 0) + t * tk) // CHUNK
                qc = ((lax.broadcasted_iota(jnp.int32, (tk, wa), 1) + c * wa) % tq + qi * tq) // CHUNK
                s = jnp.where(kc <= qc, scores(kb, c), -jnp.inf)
                m_old = m_ref[:, cols]
                m_new = jnp.maximum(m_old, jnp.max(s, axis=0, keepdims=True))
                alpha = jnp.exp2(m_old - m_new)
                p = jnp.exp2(s - m_new)
                l_ref[:, cols] = alpha * l_ref[:, cols] + jnp.sum(p, axis=0, keepdims=True)
                m_ref[:, cols] = m_new
                acc_ref[:, cols] = alpha * acc_ref[:, cols] + jnp.dot(
                    vb, p.astype(BF16), preferred_element_type=F32)
            return carry

        lax.fori_loop(0, n_blocks, block, 0)

    lam_p = lam_ref[...]
    lam = (jnp.exp(jnp.sum(lam_p[0:1] * lam_p[1:2], axis=1, keepdims=True))
           - jnp.exp(jnp.sum(lam_p[2:3] * lam_p[3:4], axis=1, keepdims=True)) + lam_init)
    acc = acc_ref[...]
    l = l_ref[...]
    out = acc[:, :tq] / l[:, :tq] - lam * (acc[:, tq:] / l[:, tq:])
    ms = jnp.mean(out * out, axis=0, keepdims=True)
    y = ((out * lax.rsqrt(ms + EPS)) * g_ref[...]) * (1.0 - lam_init)
    o_ref[...] = y.T.astype(o_ref.dtype)


def _attn(qvT, pn, lam_params, subln_col, lam_init):
    nblk = qvT.shape[0]
    S = pn.shape[0]
    tq, tk, blk = ATT_TQ, ATT_TK, ATT_BLK
    assert tk == blk and tq % (2 * tk) == 0
    k_col0 = AK_OFF // (2 * A_DH)
    return pl.pallas_call(
        functools.partial(_attn_kernel, lam_init=lam_init),
        grid=(A_HEADS, S // tq),
        in_specs=[
            pl.BlockSpec((tq // blk, 2 * A_DH, blk), lambda h, i: (i, h, 0)),
            pl.BlockSpec((S, 2 * A_DH), lambda h, i: (0, k_col0 + h)),
            pl.BlockSpec((nblk, A_DV, blk), lambda h, i: (0, A_HEADS + h, 0)),
            pl.BlockSpec((4, A_DH), lambda h, i: (0, 0)),
            pl.BlockSpec((A_DV, 1), lambda h, i: (0, 0)),
        ],
        out_specs=pl.BlockSpec((tq, A_DV), lambda h, i: (i, h)),
        out_shape=jax.ShapeDtypeStruct((S, A_WIDTH), BF16),
        scratch_shapes=[
            pltpu.VMEM((2 * A_DH, 2 * tq), BF16),
            pltpu.VMEM((2, tk, 2 * tq), BF16),
            pltpu.VMEM((1, 2 * tq), F32),
            pltpu.VMEM((1, 2 * tq), F32),
            pltpu.VMEM((A_DV, 2 * tq), F32),
        ],
        compiler_params=_cparams(("parallel", "arbitrary")),
        name="diff_attn",
    )(qvT, pn, qvT, lam_params, subln_col)


def _out_proj_kernel(hm_ref, ha_ref, w_ref, x_ref, g_ref, x1_ref, h2_ref):
    r = jnp.dot(hm_ref[...], w_ref[0:M_WIDTH, :], preferred_element_type=F32)
    r = r + jnp.dot(ha_ref[...], w_ref[M_WIDTH:, :], preferred_element_type=F32)
    x1 = x_ref[...] + r
    x1_ref[...] = x1
    ms = jnp.mean(x1 * x1, axis=-1, keepdims=True)
    h2_ref[...] = ((x1 * lax.rsqrt(ms + EPS)) * g_ref[...]).astype(h2_ref.dtype)


def _out_proj(hm, ha, w_out, x2d, g):
    S = x2d.shape[0]
    tm = OUT_TM
    return pl.pallas_call(
        _out_proj_kernel,
        grid=(S // tm,),
        in_specs=[
            pl.BlockSpec((tm, M_WIDTH), lambda i: (i, 0)),
            pl.BlockSpec((tm, A_WIDTH), lambda i: (i, 0)),
            pl.BlockSpec((M_WIDTH + A_WIDTH, D_MODEL), lambda i: (0, 0)),
            pl.BlockSpec((tm, D_MODEL), lambda i: (i, 0)),
            pl.BlockSpec((1, D_MODEL), lambda i: (0, 0)),
        ],
        out_specs=[
            pl.BlockSpec((tm, D_MODEL), lambda i: (i, 0)),
            pl.BlockSpec((tm, D_MODEL), lambda i: (i, 0)),
        ],
        out_shape=[
            jax.ShapeDtypeStruct((S, D_MODEL), F32),
            jax.ShapeDtypeStruct((S, D_MODEL), BF16),
        ],
        compiler_params=_cparams(("parallel",)),
        name="out_proj",
    )(hm, ha, w_out, x2d, g)


def _swiglu_kernel(h2_ref, wg_ref, wu_ref, wd_ref, x1_ref, fg_ref, o_ref, acc_ref):
    j = pl.program_id(1)

    @pl.when(j == 0)
    def _():
        acc_ref[...] = jnp.zeros_like(acc_ref)

    h2 = h2_ref[...]
    g = jnp.dot(h2, wg_ref[...], preferred_element_type=F32)
    u = jnp.dot(h2, wu_ref[...], preferred_element_type=F32)
    a = ((g * jax.nn.sigmoid(g)) * u).astype(BF16)
    acc_ref[...] += jnp.dot(a, wd_ref[...], preferred_element_type=F32)

    @pl.when(j == pl.num_programs(1) - 1)
    def _():
        y = x1_ref[...] + acc_ref[...]
        ms = jnp.mean(y * y, axis=-1, keepdims=True)
        o_ref[...] = (y * lax.rsqrt(ms + EPS)) * fg_ref[...]


def _swiglu(h2, wg, wu, wd, x1, fg):
    S = h2.shape[0]
    tm, tf = FF_TM, FF_TF
    return pl.pallas_call(
        _swiglu_kernel,
        grid=(S // tm, D_FF // tf),
        in_specs=[
            pl.BlockSpec((tm, D_MODEL), lambda i, j: (i, 0)),
            pl.BlockSpec((D_MODEL, tf), lambda i, j: (0, j)),
            pl.BlockSpec((D_MODEL, tf), lambda i, j: (0, j)),
            pl.BlockSpec((tf, D_MODEL), lambda i, j: (j, 0)),
            pl.BlockSpec((tm, D_MODEL), lambda i, j: (i, 0)),
            pl.BlockSpec((1, D_MODEL), lambda i, j: (0, 0)),
        ],
        out_specs=pl.BlockSpec((tm, D_MODEL), lambda i, j: (i, 0)),
        out_shape=jax.ShapeDtypeStruct((S, D_MODEL), F32),
        scratch_shapes=[pltpu.VMEM((tm, D_MODEL), F32)],
        compiler_params=_cparams(("parallel", "arbitrary")),
        name="swiglu",
    )(h2, wg, wu, wd, x1, fg)


def _rope_inv_freq():
    inv = ROPE_THETA ** (-jnp.arange(ROPE_HALF, dtype=F32) / ROPE_HALF)
    return jnp.tile(inv, LANES // ROPE_HALF).reshape(1, LANES), inv.reshape(ROPE_HALF, 1)


def kernel(x, norm1_g, w_in, conv_w, conv_b, b_igate, b_fgate, mnorm_g,
           lambda_q1, lambda_k1, lambda_q2, lambda_k2, subln_g, w_out,
           norm2_g, w_gate, w_up, w_down, final_g):
    B, S, D = x.shape
    assert (B, S, D) == (1, SEQ, D_MODEL) and norm1_g.shape[0] == 1
    l = 0
    x2d = x.reshape(S, D)

    w = w_in[l]
    o_mi = 2 * N_MQK + 2 * M_WIDTH
    o_aq = o_mi + 2 * M_HEADS
    o_ak = o_aq + A_WIDTH
    o_av = o_ak + A_WIDTH
    w_nat = jnp.concatenate([w[:, :o_mi], w[:, o_ak:o_av]], axis=1).astype(BF16)
    w_t = jnp.concatenate([w[:, o_aq:o_ak], w[:, o_av:]], axis=1).T.astype(BF16)
    w_g = jnp.pad(w[:, o_mi:o_aq], ((0, 0), (0, LANES - 2 * M_HEADS))).astype(BF16)
    b_g = jnp.pad(jnp.concatenate([b_igate[l], b_fgate[l]]), (0, LANES - 2 * M_HEADS)).reshape(1, LANES)

    inv_row, inv_col = _rope_inv_freq()

    gc, gr, pn, qvT = _front(x2d, norm1_g[l].reshape(1, D), w_g, b_g, w_nat, w_t,
                             inv_row, inv_col, conv_w[l], conv_b[l].reshape(1, -1))

    hm = _mlstm(pn, gc, gr, mnorm_g[l].reshape(1, M_WIDTH))

    lam_init = 0.8 - 0.6 * math.exp(-0.3 * l)
    lam_params = jnp.stack([lambda_q1[l], lambda_k1[l], lambda_q2[l], lambda_k2[l]]).astype(F32)
    ha = _attn(qvT, pn, lam_params, subln_g[l].reshape(A_DV, 1), lam_init)

    x1, h2 = _out_proj(hm, ha, w_out[l].astype(BF16), x2d, norm2_g[l].reshape(1, D))
    out = _swiglu(h2, w_gate[l].astype(BF16), w_up[l].astype(BF16), w_down[l].astype(BF16),
                  x1, final_g.reshape(1, D))
    return out.reshape(B, S, D)
```

```python
import functools
import math

import jax
import jax.numpy as jnp
from jax import lax
from jax.experimental import pallas as pl
from jax.experimental.pallas import tpu as pltpu

F32 = jnp.float32
BF16 = jnp.bfloat16

D_MODEL = 2048
SEQ = 16384
CHUNK = 64
EPS = 1e-6
ROPE_THETA = 10000.0

M_HEADS = 4
M_DV = 256
M_DQK = 128
M_CONV = 4
GATE_CAP = 15.0
M_WIDTH = M_HEADS * M_DV
N_MQK = M_HEADS * M_DQK

A_HEADS = 8
A_DH = 64
A_DV = 128
A_WIDTH = A_HEADS * A_DV
ROPE_HALF = A_DH // 2

D_FF = 5632

LANES = 128
VMEM_LIMIT = 56 * 1024 * 1024

FRONT_TM = 512
FRONT_TN = 512
ATT_BLK = 512
ATT_TQ = 2048
ATT_TK = 512
LOG2E = math.log2(math.e)
ATT_L_LIMIT = 2.0 ** 60
ML_L = 128
ML_R = 1024
OUT_TM = 512
FF_TM = 512
FF_TF = 512

N_NAT = 2 * N_MQK + 2 * M_WIDTH + A_WIDTH
AK_OFF = 2 * N_MQK + 2 * M_WIDTH
ONES_W = LANES


def _cparams(sem):
    return pltpu.CompilerParams(dimension_semantics=sem, vmem_limit_bytes=VMEM_LIMIT)


def _front_kernel(x_ref, g_ref, wg_ref, bg_ref, wn_ref, wt_ref, inv_row_ref, inv_col_ref, cw_ref, cb_ref,
                  gc_ref, gr_ref, pn_ref, qv_ref, h_ref, ybuf, tail_ref):
    tm = x_ref.shape[0]
    row0 = pl.program_id(0) * tm

    @pl.when(pl.program_id(0) == 0)
    def _():
        tail_ref[...] = jnp.zeros_like(tail_ref)

    x = x_ref[...]
    ms = jnp.mean(x * x, axis=-1, keepdims=True)
    h_ref[...] = ((x * lax.rsqrt(ms + EPS)) * g_ref[...]).astype(BF16)
    h = h_ref[...]

    pre = jnp.dot(h, wg_ref[...], preferred_element_type=F32) + bg_ref[...]
    capped = GATE_CAP * jnp.tanh(pre / GATE_CAP)
    logf = -(jnp.maximum(-capped, 0.0) + jnp.log1p(jnp.exp(-jnp.abs(capped))))
    lane = lax.broadcasted_iota(jnp.int32, pre.shape, 1)
    gates = jnp.where(lane < M_HEADS, capped, logf)
    gc_ref[...] = gates[:, :2 * M_HEADS]
    gates_t = gates.T[:2 * M_HEADS, :]
    for b in range(tm // LANES):
        gr_ref[b] = gates_t[:, b * LANES:(b + 1) * LANES]

    pos_c = (row0 + lax.broadcasted_iota(jnp.int32, (tm, LANES), 0)).astype(F32)
    ang = pos_c * inv_row_ref[...]
    ln = lax.broadcasted_iota(jnp.int32, (tm, LANES), 1)
    cos_rm = jnp.cos(ang)
    sin_rm = jnp.where((ln % A_DH) < ROPE_HALF, -jnp.sin(ang), jnp.sin(ang))
    pos_r = (row0 + lax.broadcasted_iota(jnp.int32, (ROPE_HALF, tm), 1)).astype(F32)
    ang_t = inv_col_ref[...] * pos_r
    cT = jnp.cos(ang_t)
    sT = jnp.sin(ang_t)

    tn = FRONT_TN
    reps = tn // LANES
    for j in range(N_NAT // tn):
        cols = slice(j * tn, (j + 1) * tn)
        r = jnp.dot(h, wn_ref[:, cols], preferred_element_type=F32)
        if (j + 1) * tn <= 2 * N_MQK:
            ybuf[0:8, :] = tail_ref[j]
            ybuf[8:8 + tm, :] = r
            tail_ref[j] = r[tm - 8:tm, :]
            y = cb_ref[:, cols]
            for tap in range(M_CONV - 1):
                off = 8 - (M_CONV - 1) + tap
                y = y + ybuf[off:off + tm, :] * cw_ref[tap:tap + 1, cols]
            y = y + r * cw_ref[M_CONV - 1:M_CONV, cols]
            r = y * jax.nn.sigmoid(y)
            if (j + 1) * tn <= N_MQK:
                r = r * (M_DQK ** -0.5)
        elif j * tn >= AK_OFF:
            c = jnp.concatenate([cos_rm] * reps, axis=1)
            s = jnp.concatenate([sin_rm] * reps, axis=1)
            first = (lax.broadcasted_iota(jnp.int32, r.shape, 1) % A_DH) < ROPE_HALF
            swapped = jnp.where(first, pltpu.roll(r, tn - ROPE_HALF, 1), pltpu.roll(r, ROPE_HALF, 1))
            r = r * c + swapped * s
        pn_ref[:, cols] = r.astype(pn_ref.dtype)

    rows = 256
    scale = A_DH ** -0.5 * LOG2E
    for rb in range(wt_ref.shape[0] // rows):
        r = lax.dot_general(wt_ref[rb * rows:(rb + 1) * rows, :], h,
                            (((1,), (1,)), ((), ())), preferred_element_type=F32)
        if rb * rows < A_WIDTH:
            parts = []
            for g in range(rows // A_DH):
                t1 = r[g * A_DH:g * A_DH + ROPE_HALF]
                t2 = r[g * A_DH + ROPE_HALF:(g + 1) * A_DH]
                parts.append((t1 * cT - t2 * sT) * scale)
                parts.append((t2 * cT + t1 * sT) * scale)
            r = jnp.concatenate(parts, axis=0)
        qv_ref[0, rb * rows:(rb + 1) * rows, :] = r.astype(qv_ref.dtype)


def _front(x2d, g, wg, bg, w_nat, w_t, inv_row, inv_col, conv_w, conv_b):
    S = x2d.shape[0]
    tm = FRONT_TM
    assert tm == ATT_BLK and FRONT_TN == N_MQK
    nfeat = w_t.shape[0]
    const = lambda i: (0, 0)
    resident = dict(pipeline_mode=pl.Buffered(1))
    return pl.pallas_call(
        _front_kernel,
        grid=(S // tm,),
        in_specs=[
            pl.BlockSpec((tm, D_MODEL), lambda i: (i, 0)),
            pl.BlockSpec((1, D_MODEL), const),
            pl.BlockSpec((D_MODEL, LANES), const),
            pl.BlockSpec((1, LANES), const),
            pl.BlockSpec((D_MODEL, N_NAT), const, **resident),
            pl.BlockSpec((nfeat, D_MODEL), const, **resident),
            pl.BlockSpec((1, LANES), const),
            pl.BlockSpec((ROPE_HALF, 1), const),
            pl.BlockSpec((M_CONV, 2 * N_MQK), const),
            pl.BlockSpec((1, 2 * N_MQK), const),
        ],
        out_specs=[
            pl.BlockSpec((tm, 2 * M_HEADS), lambda i: (i, 0)),
            pl.BlockSpec((tm // LANES, 2 * M_HEADS, LANES), lambda i: (i, 0, 0)),
            pl.BlockSpec((tm, N_NAT), lambda i: (i, 0)),
            pl.BlockSpec((1, nfeat, ATT_BLK), lambda i: (i, 0, 0)),
        ],
        out_shape=[
            jax.ShapeDtypeStruct((S, 2 * M_HEADS), F32),
            jax.ShapeDtypeStruct((S // LANES, 2 * M_HEADS, LANES), F32),
            jax.ShapeDtypeStruct((S, N_NAT), BF16),
            jax.ShapeDtypeStruct((S // ATT_BLK, nfeat, ATT_BLK), BF16),
        ],
        scratch_shapes=[
            pltpu.VMEM((tm, D_MODEL), BF16),
            pltpu.VMEM((tm + 8, FRONT_TN), F32),
            pltpu.VMEM((2 * N_MQK // FRONT_TN, 8, FRONT_TN), F32),
        ],
        compiler_params=_cparams(("arbitrary",)),
        name="front",
    )(x2d, g, wg, bg, w_nat, w_t, inv_row, inv_col, conv_w, conv_b)


def _split3(x):
    hi = x.astype(BF16)
    r1 = x - hi.astype(F32)
    mid = r1.astype(BF16)
    lo = (r1 - mid.astype(F32)).astype(BF16)
    return hi, mid, lo


def _mlstm_kernel(pm_ref, gc_ref, gr_ref, mg_ref, o_ref, c_ref, m_ref):
    R = pm_ref.shape[0]
    L = ML_L
    n_chunks = R // L
    NQK = 2 * N_MQK

    @pl.when(pl.program_id(0) == 0)
    def _():
        c_ref[...] = jnp.zeros_like(c_ref)
        m_ref[...] = jnp.zeros_like(m_ref)

    t_idx = lax.broadcasted_iota(jnp.int32, (L, L), 0)
    s_idx = lax.broadcasted_iota(jnp.int32, (L, L), 1)
    causal = s_idx <= t_idx
    tril = causal.astype(BF16)
    triu = (t_idx <= s_idx).astype(BF16)
    ones_ext = jnp.ones((L, ONES_W), BF16)

    for c in range(n_chunks):
        rows = slice(c * L, (c + 1) * L)
        gc = gc_ref[rows, :]
        gr = gr_ref[c]
        b_col_all = sum(jnp.dot(tril, p, preferred_element_type=F32) for p in _split3(gc))
        b_row_all = sum(jnp.dot(p, triu, preferred_element_type=F32) for p in _split3(gr))
        H = range(M_HEADS)
        qh = [pm_ref[rows, hd * M_DQK:(hd + 1) * M_DQK] for hd in H]
        kh = [pm_ref[rows, N_MQK + hd * M_DQK:N_MQK + (hd + 1) * M_DQK] for hd in H]
        v_ext = [jnp.concatenate([pm_ref[rows, NQK + hd * M_DV:NQK + (hd + 1) * M_DV], ones_ext], axis=1)
                 for hd in H]
        i_row = [gr[hd:hd + 1, :] for hd in H]
        b_row = [b_row_all[M_HEADS + hd:M_HEADS + hd + 1, :] for hd in H]
        b_col = [b_col_all[:, M_HEADS + hd:M_HEADS + hd + 1] for hd in H]

        logd = [jnp.where(causal, b_col[h] - b_row[h] + i_row[h], -jnp.inf) for h in H]
        m_loc = [jnp.max(logd[h], axis=1, keepdims=True) for h in H]
        d_loc = [jnp.exp(logd[h] - m_loc[h]) for h in H]
        s = [lax.dot_general(qh[h], kh[h], (((1,), (1,)), ((), ())), preferred_element_type=F32) for h in H]
        sd = [(s[h] * d_loc[h]).astype(BF16) for h in H]
        p_ext = [jnp.dot(sd[h], v_ext[h], preferred_element_type=F32) for h in H]

        b_last = [b_row[h][:, L - 1:L] for h in H]
        w_log = [b_last[h] - b_row[h] + i_row[h] for h in H]
        a_loc = [jnp.max(w_log[h], axis=1, keepdims=True) for h in H]
        kw_t = [(kh[h].astype(F32).T * jnp.exp(w_log[h] - a_loc[h])).astype(BF16) for h in H]
        kv_ext = [jnp.dot(kw_t[h], v_ext[h], preferred_element_type=F32) for h in H]

        m_prev = [m_ref[hd:hd + 1, 0:1] for hd in H]
        c_prev = [c_ref[hd] for hd in H]
        inter = [b_col[h] + m_prev[h] for h in H]
        m_t = [jnp.maximum(inter[h], m_loc[h]) for h in H]
        qc = [jnp.dot(qh[h], c_prev[h].astype(BF16), preferred_element_type=F32) for h in H]
        nd = [jnp.exp(inter[h] - m_t[h]) * qc[h] + jnp.exp(m_loc[h] - m_t[h]) * p_ext[h] for h in H]
        hh = [nd[h][:, :M_DV] / jnp.maximum(jnp.abs(nd[h][:, M_DV:M_DV + 1]), jnp.exp(-m_t[h])) for h in H]

        m_new = [jnp.maximum(b_last[h] + m_prev[h], a_loc[h]) for h in H]
        for h in H:
            c_ref[h] = (jnp.exp(b_last[h] + m_prev[h] - m_new[h]) * c_prev[h]
                        + jnp.exp(a_loc[h] - m_new[h]) * kv_ext[h])
            m_ref[h:h + 1, :] = jnp.broadcast_to(m_new[h], (1, m_ref.shape[1]))

        ms = [jnp.mean(hh[h] * hh[h], axis=-1, keepdims=True) for h in H]
        for h in H:
            cols = slice(h * M_DV, (h + 1) * M_DV)
            hn = (hh[h] * lax.rsqrt(ms[h] + EPS)) * mg_ref[:, cols]
            mo = pm_ref[rows, NQK + M_WIDTH + h * M_DV:NQK + M_WIDTH + (h + 1) * M_DV].astype(F32)
            o_ref[rows, cols] = (hn * jax.nn.sigmoid(mo)).astype(o_ref.dtype)


def _mlstm(pn, gc, gr, mnorm_g):
    S = pn.shape[0]
    R = ML_R
    wm = 2 * N_MQK + 2 * M_WIDTH
    return pl.pallas_call(
        _mlstm_kernel,
        grid=(S // R,),
        in_specs=[
            pl.BlockSpec((R, wm), lambda i: (i, 0)),
            pl.BlockSpec((R, 2 * M_HEADS), lambda i: (i, 0)),
            pl.BlockSpec((R // ML_L, 2 * M_HEADS, ML_L), lambda i: (i, 0, 0)),
            pl.BlockSpec((1, M_WIDTH), lambda i: (0, 0)),
        ],
        out_specs=pl.BlockSpec((R, M_WIDTH), lambda i: (i, 0)),
        out_shape=jax.ShapeDtypeStruct((S, M_WIDTH), BF16),
        scratch_shapes=[
            pltpu.VMEM((M_HEADS, M_DQK, M_DV + ONES_W), F32),
            pltpu.VMEM((8, LANES), F32),
        ],
        compiler_params=_cparams(("arbitrary",)),
        name="mlstm",
    )(pn, gc, gr, mnorm_g)


def _attn_kernel(qT_ref, k_ref, vT_ref, lam_ref, g_ref, o_ref,
                 qz_ref, p_ref, m_ref, l_ref, acc_ref, *, lam_init):
    qi = pl.program_id(1)
    tq, tk, blk = ATT_TQ, ATT_TK, ATT_BLK
    n_blocks = (qi + 1) * (tq // tk)
    ncol = 2 * tq
    wa = 256

    frow = lax.broadcasted_iota(jnp.int32, (2 * A_DH, blk), 0)
    for b in range(tq // blk):
        qb = qT_ref[b]
        zero = jnp.zeros_like(qb)
        qz_ref[:, b * blk:(b + 1) * blk] = jnp.where(frow < A_DH, qb, zero)
        qz_ref[:, tq + b * blk:tq + (b + 1) * blk] = jnp.where(frow >= A_DH, qb, zero)

    def key_block(t):
        return k_ref[pl.ds(pl.multiple_of(t * tk, tk), tk), :]

    def diag_visible(c, d):
        kc = (lax.broadcasted_iota(jnp.int32, (tk, wa), 0) + d * tk) // CHUNK
        qc = ((lax.broadcasted_iota(jnp.int32, (tk, wa), 1) + c * wa) % tq) // CHUNK
        return kc <= qc

    def scores(kb, c):
        return jnp.dot(kb, qz_ref[:, c * wa:(c + 1) * wa], preferred_element_type=F32)

    all_chunks = tuple(range(ncol // wa))
    n_diag = tq // tk
    first_diag = qi * n_diag

    def chunks_from(d):
        return tuple(c for c in all_chunks if (c * wa) % tq >= d * tk)

    k_first = k_ref[0:CHUNK, :]
    for c in all_chunks:
        cols = slice(c * wa, (c + 1) * wa)
        m_ref[:, cols] = jnp.max(scores(k_first, c), axis=0, keepdims=True)
    l_ref[...] = jnp.zeros_like(l_ref)
    acc_ref[...] = jnp.zeros_like(acc_ref)

    kb0 = key_block(0)
    for c in all_chunks:
        cols = slice(c * wa, (c + 1) * wa)
        vis = jnp.logical_or(diag_visible(c, 0), qi > 0)
        p = jnp.exp2(jnp.where(vis, scores(kb0, c), -jnp.inf) - m_ref[:, cols])
        l_ref[:, cols] += jnp.sum(p, axis=0, keepdims=True)
        p_ref[0, :, cols] = p.astype(BF16)

    def stage_ab(t, par, diag=None, chunks=all_chunks):
        kb = key_block(t)
        for c in chunks:
            cols = slice(c * wa, (c + 1) * wa)
            s = scores(kb, c)
            if diag is not None:
                s = jnp.where(diag_visible(c, diag), s, -jnp.inf)
            p = jnp.exp2(s - m_ref[:, cols])
            l_ref[:, cols] += jnp.sum(p, axis=0, keepdims=True)
            p_ref[par, :, cols] = p.astype(BF16)

    def stage_c(t, par, chunks=all_chunks):
        vb = vT_ref[t]
        for c in chunks:
            cols = slice(c * wa, (c + 1) * wa)
            acc_ref[:, cols] += jnp.dot(vb, p_ref[par, :, cols], preferred_element_type=F32)

    @pl.when(qi > 0)
    def _():
        def pair(t):
            stage_c(t, 0)
            stage_ab(t + 1, 1)
            stage_c(t + 1, 1)
            stage_ab(t + 2, 0)

        def two_pairs(v, carry):
            pair(4 * v)
            pair(4 * v + 2)
            return carry

        n_pairs = lax.shift_right_logical(first_diag, 1) - 1
        lax.fori_loop(0, lax.shift_right_logical(n_pairs, 1), two_pairs, 0)

        @pl.when(jnp.bitwise_and(n_pairs, 1) == 1)
        def _():
            pair(2 * (n_pairs - 1))

        stage_c(first_diag - 2, 0)
        stage_ab(first_diag - 1, 1)
        stage_c(first_diag - 1, 1)
        stage_ab(first_diag, 0, diag=0)

    for d in range(1, n_diag):
        stage_c(first_diag + d - 1, (d - 1) % 2, chunks=chunks_from(d - 1))
        stage_ab(first_diag + d, d % 2, diag=d, chunks=chunks_from(d))
    stage_c(first_diag + n_diag - 1, (n_diag - 1) % 2, chunks=chunks_from(n_diag - 1))

    l_fast = l_ref[...]
    overflow = jnp.max(jnp.where(l_fast < ATT_L_LIMIT, 0.0, 1.0)) > 0.5

    @pl.when(overflow)
    def _():
        m_ref[...] = jnp.full_like(m_ref, -jnp.inf)
        l_ref[...] = jnp.zeros_like(l_ref)
        acc_ref[...] = jnp.zeros_like(acc_ref)

        def block(t, carry):
            kb = key_block(t)
            vb = vT_ref[t]
            for c in range(ncol // wa):
                cols = slice(c * wa, (c + 1) * wa)
                kc = (lax.broadcasted_iota(jnp.int32, (tk, wa), 0) + t * tk) // CHUNK
                qc = ((lax.broadcasted_iota(jnp.int32, (tk, wa), 1) + c * wa) % tq + qi * tq) // CHUNK
                s = jnp.where(kc <= qc, scores(kb, c), -jnp.inf)
                m_old = m_ref[:, cols]
                m_new = jnp.maximum(m_old, jnp.max(s, axis=0, keepdims=True))
                alpha = jnp.exp2(m_old - m_new)
                p = jnp.exp2(s - m_new)
                l_ref[:, cols] = alpha * l_ref[:, cols] + jnp.sum(p, axis=0, keepdims=True)
                m_ref[:, cols] = m_new
                acc_ref[:, cols] = alpha * acc_ref[:, cols] + jnp.dot(
                    vb, p.astype(BF16), preferred_element_type=F32)
            return carry

        lax.fori_loop(0, n_blocks, block, 0)

    lam_p = lam_ref[...]
    lam = (jnp.exp(jnp.sum(lam_p[0:1] * lam_p[1:2], axis=1, keepdims=True))
           - jnp.exp(jnp.sum(lam_p[2:3] * lam_p[3:4], axis=1, keepdims=True)) + lam_init)
    acc = acc_ref[...]
    l = l_ref[...]
    out = acc[:, :tq] / l[:, :tq] - lam * (acc[:, tq:] / l[:, tq:])
    ms = jnp.mean(out * out, axis=0, keepdims=True)
    y = ((out * lax.rsqrt(ms + EPS)) * g_ref[...]) * (1.0 - lam_init)
    o_ref[...] = y.T.astype(o_ref.dtype)


def _attn(qvT, pn, lam_params, subln_col, lam_init):
    nblk = qvT.shape[0]
    S = pn.shape[0]
    tq, tk, blk = ATT_TQ, ATT_TK, ATT_BLK
    assert tk == blk and tq % (2 * tk) == 0
    k_col0 = AK_OFF // (2 * A_DH)
    return pl.pallas_call(
        functools.partial(_attn_kernel, lam_init=lam_init),
        grid=(A_HEADS, S // tq),
        in_specs=[
            pl.BlockSpec((tq // blk, 2 * A_DH, blk), lambda h, i: (i, h, 0)),
            pl.BlockSpec((S, 2 * A_DH), lambda h, i: (0, k_col0 + h)),
            pl.BlockSpec((nblk, A_DV, blk), lambda h, i: (0, A_HEADS + h, 0)),
            pl.BlockSpec((4, A_DH), lambda h, i: (0, 0)),
            pl.BlockSpec((A_DV, 1), lambda h, i: (0, 0)),
        ],
        out_specs=pl.BlockSpec((tq, A_DV), lambda h, i: (i, h)),
        out_shape=jax.ShapeDtypeStruct((S, A_WIDTH), BF16),
        scratch_shapes=[
            pltpu.VMEM((2 * A_DH, 2 * tq), BF16),
            pltpu.VMEM((2, tk, 2 * tq), BF16),
            pltpu.VMEM((1, 2 * tq), F32),
            pltpu.VMEM((1, 2 * tq), F32),
            pltpu.VMEM((A_DV, 2 * tq), F32),
        ],
        compiler_params=_cparams(("parallel", "arbitrary")),
        name="diff_attn",
    )(qvT, pn, qvT, lam_params, subln_col)


def _out_proj_kernel(hm_ref, ha_ref, w_ref, x_ref, g_ref, x1_ref, h2_ref):
    r = jnp.dot(hm_ref[...], w_ref[0:M_WIDTH, :], preferred_element_type=F32)
    r = r + jnp.dot(ha_ref[...], w_ref[M_WIDTH:, :], preferred_element_type=F32)
    x1 = x_ref[...] + r
    x1_ref[...] = x1
    ms = jnp.mean(x1 * x1, axis=-1, keepdims=True)
    h2_ref[...] = ((x1 * lax.rsqrt(ms + EPS)) * g_ref[...]).astype(h2_ref.dtype)


def _out_proj(hm, ha, w_out, x2d, g):
    S = x2d.shape[0]
    tm = OUT_TM
    return pl.pallas_call(
        _out_proj_kernel,
        grid=(S // tm,),
        in_specs=[
            pl.BlockSpec((tm, M_WIDTH), lambda i: (i, 0)),
            pl.BlockSpec((tm, A_WIDTH), lambda i: (i, 0)),
            pl.BlockSpec((M_WIDTH + A_WIDTH, D_MODEL), lambda i: (0, 0)),
            pl.BlockSpec((tm, D_MODEL), lambda i: (i, 0)),
            pl.BlockSpec((1, D_MODEL), lambda i: (0, 0)),
        ],
        out_specs=[
            pl.BlockSpec((tm, D_MODEL), lambda i: (i, 0)),
            pl.BlockSpec((tm, D_MODEL), lambda i: (i, 0)),
        ],
        out_shape=[
            jax.ShapeDtypeStruct((S, D_MODEL), F32),
            jax.ShapeDtypeStruct((S, D_MODEL), BF16),
        ],
        compiler_params=_cparams(("parallel",)),
        name="out_proj",
    )(hm, ha, w_out, x2d, g)


def _swiglu_kernel(h2_ref, wg_ref, wu_ref, wd_ref, x1_ref, fg_ref, o_ref, acc_ref):
    j = pl.program_id(1)

    @pl.when(j == 0)
    def _():
        acc_ref[...] = jnp.zeros_like(acc_ref)

    h2 = h2_ref[...]
    g = jnp.dot(h2, wg_ref[...], preferred_element_type=F32)
    u = jnp.dot(h2, wu_ref[...], preferred_element_type=F32)
    a = ((g * jax.nn.sigmoid(g)) * u).astype(BF16)
    acc_ref[...] += jnp.dot(a, wd_ref[...], preferred_element_type=F32)

    @pl.when(j == pl.num_programs(1) - 1)
    def _():
        y = x1_ref[...] + acc_ref[...]
        ms = jnp.mean(y * y, axis=-1, keepdims=True)
        o_ref[...] = (y * lax.rsqrt(ms + EPS)) * fg_ref[...]


def _swiglu(h2, wg, wu, wd, x1, fg):
    S = h2.shape[0]
    tm, tf = FF_TM, FF_TF
    return pl.pallas_call(
        _swiglu_kernel,
        grid=(S // tm, D_FF // tf),
        in_specs=[
            pl.BlockSpec((tm, D_MODEL), lambda i, j: (i, 0)),
            pl.BlockSpec((D_MODEL, tf), lambda i, j: (0, j)),
            pl.BlockSpec((D_MODEL, tf), lambda i, j: (0, j)),
            pl.BlockSpec((tf, D_MODEL), lambda i, j: (j, 0)),
            pl.BlockSpec((tm, D_MODEL), lambda i, j: (i, 0)),
            pl.BlockSpec((1, D_MODEL), lambda i, j: (0, 0)),
        ],
        out_specs=pl.BlockSpec((tm, D_MODEL), lambda i, j: (i, 0)),
        out_shape=jax.ShapeDtypeStruct((S, D_MODEL), F32),
        scratch_shapes=[pltpu.VMEM((tm, D_MODEL), F32)],
        compiler_params=_cparams(("parallel", "arbitrary")),
        name="swiglu",
    )(h2, wg, wu, wd, x1, fg)


def _rope_inv_freq():
    inv = ROPE_THETA ** (-jnp.arange(ROPE_HALF, dtype=F32) / ROPE_HALF)
    return jnp.tile(inv, LANES // ROPE_HALF).reshape(1, LANES), inv.reshape(ROPE_HALF, 1)


def kernel(x, norm1_g, w_in, conv_w, conv_b, b_igate, b_fgate, mnorm_g,
           lambda_q1, lambda_k1, lambda_q2, lambda_k2, subln_g, w_out,
           norm2_g, w_gate, w_up, w_down, final_g):
    B, S, D = x.shape
    assert (B, S, D) == (1, SEQ, D_MODEL) and norm1_g.shape[0] == 1
    l = 0
    x2d = x.reshape(S, D)

    w = w_in[l]
    o_mi = 2 * N_MQK + 2 * M_WIDTH
    o_aq = o_mi + 2 * M_HEADS
    o_ak = o_aq + A_WIDTH
    o_av = o_ak + A_WIDTH
    w = lax.optimization_barrier(w.astype(BF16))
    w_nat = jnp.concatenate([w[:, :o_mi], w[:, o_ak:o_av]], axis=1)
    w_t = jnp.concatenate([w[:, o_aq:o_ak], w[:, o_av:]], axis=1).T
    w_g = jnp.pad(w[:, o_mi:o_aq], ((0, 0), (0, LANES - 2 * M_HEADS)))
    b_g = jnp.pad(jnp.concatenate([b_igate[l], b_fgate[l]]), (0, LANES - 2 * M_HEADS)).reshape(1, LANES)

    inv_row, inv_col = _rope_inv_freq()

    gc, gr, pn, qvT = _front(x2d, norm1_g[l].reshape(1, D), w_g, b_g, w_nat, w_t,
                             inv_row, inv_col, conv_w[l], conv_b[l].reshape(1, -1))

    hm = _mlstm(pn, gc, gr, mnorm_g[l].reshape(1, M_WIDTH))

    lam_init = 0.8 - 0.6 * math.exp(-0.3 * l)
    lam_params = jnp.stack([lambda_q1[l], lambda_k1[l], lambda_q2[l], lambda_k2[l]]).astype(F32)
    ha = _attn(qvT, pn, lam_params, subln_g[l].reshape(A_DV, 1), lam_init)

    x1, h2 = _out_proj(hm, ha, w_out[l].astype(BF16), x2d, norm2_g[l].reshape(1, D))
    out = _swiglu(h2, w_gate[l].astype(BF16), w_up[l].astype(BF16), w_down[l].astype(BF16),
                  x1, final_g.reshape(1, D))
    return out.reshape(B, S, D)
```

```python
import functools
import math

import jax
import jax.numpy as jnp
from jax import lax
from jax.experimental import pallas as pl
from jax.experimental.pallas import tpu as pltpu

F32 = jnp.float32
BF16 = jnp.bfloat16

D_MODEL = 2048
SEQ = 16384
CHUNK = 64
EPS = 1e-6
ROPE_THETA = 10000.0

M_HEADS = 4
M_DV = 256
M_DQK = 128
M_CONV = 4
GATE_CAP = 15.0
M_WIDTH = M_HEADS * M_DV
N_MQK = M_HEADS * M_DQK

A_HEADS = 8
A_DH = 64
A_DV = 128
A_WIDTH = A_HEADS * A_DV
ROPE_HALF = A_DH // 2

D_FF = 5632

LANES = 128
VMEM_LIMIT = 56 * 1024 * 1024

FRONT_TM = 512
FRONT_TN = 512
ATT_BLK = 512
ATT_TQ = 2048
ATT_TK = 512
LOG2E = math.log2(math.e)
ATT_L_LIMIT = 2.0 ** 60
ML_L = 128
ML_R = 1024
WPREP_TR = 256
ML_GROUP = 1
OUT_TM = 512
FF_TM = 512
FF_TF = 512

N_NAT = 2 * N_MQK + 2 * M_WIDTH + A_WIDTH
AK_OFF = 2 * N_MQK + 2 * M_WIDTH
ONES_W = LANES


def _cparams(sem):
    return pltpu.CompilerParams(dimension_semantics=sem, vmem_limit_bytes=VMEM_LIMIT)


def _front_kernel(x_ref, g_ref, wg_ref, bg_ref, wn_ref, wt_ref, inv_row_ref, inv_col_ref, cw_ref, cb_ref,
                  gc_ref, gr_ref, pn_ref, qv_ref, h_ref, ybuf, tail_ref):
    tm = x_ref.shape[0]
    row0 = pl.program_id(0) * tm

    @pl.when(pl.program_id(0) == 0)
    def _():
        tail_ref[...] = jnp.zeros_like(tail_ref)

    x = x_ref[...]
    ms = jnp.mean(x * x, axis=-1, keepdims=True)
    h_ref[...] = ((x * lax.rsqrt(ms + EPS)) * g_ref[...]).astype(BF16)
    h = h_ref[...]

    pre = jnp.dot(h, wg_ref[...], preferred_element_type=F32) + bg_ref[...]
    capped = GATE_CAP * jnp.tanh(pre / GATE_CAP)
    logf = -(jnp.maximum(-capped, 0.0) + jnp.log1p(jnp.exp(-jnp.abs(capped))))
    lane = lax.broadcasted_iota(jnp.int32, pre.shape, 1)
    gates = jnp.where(lane < M_HEADS, capped, logf)
    gc_ref[...] = gates[:, :2 * M_HEADS]
    gates_t = gates.T[:2 * M_HEADS, :]
    for b in range(tm // LANES):
        gr_ref[b] = gates_t[:, b * LANES:(b + 1) * LANES]

    def row_major_tables():
        pos_c = (row0 + lax.broadcasted_iota(jnp.int32, (tm, LANES), 0)).astype(F32)
        ang = pos_c * inv_row_ref[...]
        ln = lax.broadcasted_iota(jnp.int32, (tm, LANES), 1)
        return jnp.cos(ang), jnp.where((ln % A_DH) < ROPE_HALF, -jnp.sin(ang), jnp.sin(ang))

    def feature_major_tables():
        pos_r = (row0 + lax.broadcasted_iota(jnp.int32, (ROPE_HALF, tm), 1)).astype(F32)
        ang_t = inv_col_ref[...] * pos_r
        return jnp.cos(ang_t), jnp.sin(ang_t)

    tn = FRONT_TN
    reps = tn // LANES
    cos_rm = sin_rm = None
    for j in range(N_NAT // tn):
        cols = slice(j * tn, (j + 1) * tn)
        r = jnp.dot(h, wn_ref[:, cols], preferred_element_type=F32)
        if (j + 1) * tn <= 2 * N_MQK:
            ybuf[0:8, :] = tail_ref[j]
            ybuf[8:8 + tm, :] = r
            tail_ref[j] = r[tm - 8:tm, :]
            y = cb_ref[:, cols]
            for tap in range(M_CONV - 1):
                off = 8 - (M_CONV - 1) + tap
                y = y + ybuf[off:off + tm, :] * cw_ref[tap:tap + 1, cols]
            y = y + r * cw_ref[M_CONV - 1:M_CONV, cols]
            r = y * jax.nn.sigmoid(y)
            if (j + 1) * tn <= N_MQK:
                r = r * (M_DQK ** -0.5)
        elif j * tn >= AK_OFF:
            if cos_rm is None:
                cos_rm, sin_rm = row_major_tables()
            c = jnp.concatenate([cos_rm] * reps, axis=1)
            s = jnp.concatenate([sin_rm] * reps, axis=1)
            first = (lax.broadcasted_iota(jnp.int32, r.shape, 1) % A_DH) < ROPE_HALF
            swapped = jnp.where(first, pltpu.roll(r, tn - ROPE_HALF, 1), pltpu.roll(r, ROPE_HALF, 1))
            r = r * c + swapped * s
        pn_ref[:, cols] = r.astype(pn_ref.dtype)

    rows = 256
    scale = A_DH ** -0.5 * LOG2E
    cT, sT = feature_major_tables()
    for rb in range(wt_ref.shape[0] // rows):
        r = lax.dot_general(wt_ref[rb * rows:(rb + 1) * rows, :], h,
                            (((1,), (1,)), ((), ())), preferred_element_type=F32)
        if rb * rows < A_WIDTH:
            parts = []
            for g in range(rows // A_DH):
                t1 = r[g * A_DH:g * A_DH + ROPE_HALF]
                t2 = r[g * A_DH + ROPE_HALF:(g + 1) * A_DH]
                parts.append((t1 * cT - t2 * sT) * scale)
                parts.append((t2 * cT + t1 * sT) * scale)
            r = jnp.concatenate(parts, axis=0)
        qv_ref[0, rb * rows:(rb + 1) * rows, :] = r.astype(qv_ref.dtype)


def _front(x2d, g, wg, bg, w_nat, w_t, inv_row, inv_col, conv_w, conv_b):
    S = x2d.shape[0]
    tm = FRONT_TM
    assert tm == ATT_BLK and FRONT_TN == N_MQK
    nfeat = w_t.shape[0]
    const = lambda i: (0, 0)
    resident = dict(pipeline_mode=pl.Buffered(1))
    return pl.pallas_call(
        _front_kernel,
        grid=(S // tm,),
        in_specs=[
            pl.BlockSpec((tm, D_MODEL), lambda i: (i, 0)),
            pl.BlockSpec((1, D_MODEL), const),
            pl.BlockSpec((D_MODEL, LANES), const),
            pl.BlockSpec((1, LANES), const),
            pl.BlockSpec((D_MODEL, N_NAT), const, **resident),
            pl.BlockSpec((nfeat, D_MODEL), const, **resident),
            pl.BlockSpec((1, LANES), const),
            pl.BlockSpec((ROPE_HALF, 1), const),
            pl.BlockSpec((M_CONV, 2 * N_MQK), const),
            pl.BlockSpec((1, 2 * N_MQK), const),
        ],
        out_specs=[
            pl.BlockSpec((tm, 2 * M_HEADS), lambda i: (i, 0)),
            pl.BlockSpec((tm // LANES, 2 * M_HEADS, LANES), lambda i: (i, 0, 0)),
            pl.BlockSpec((tm, N_NAT), lambda i: (i, 0)),
            pl.BlockSpec((1, nfeat, ATT_BLK), lambda i: (i, 0, 0)),
        ],
        out_shape=[
            jax.ShapeDtypeStruct((S, 2 * M_HEADS), F32),
            jax.ShapeDtypeStruct((S // LANES, 2 * M_HEADS, LANES), F32),
            jax.ShapeDtypeStruct((S, N_NAT), BF16),
            jax.ShapeDtypeStruct((S // ATT_BLK, nfeat, ATT_BLK), BF16),
        ],
        scratch_shapes=[
            pltpu.VMEM((tm, D_MODEL), BF16),
            pltpu.VMEM((tm + 8, FRONT_TN), F32),
            pltpu.VMEM((2 * N_MQK // FRONT_TN, 8, FRONT_TN), F32),
        ],
        compiler_params=_cparams(("arbitrary",)),
        name="front",
    )(x2d, g, wg, bg, w_nat, w_t, inv_row, inv_col, conv_w, conv_b)


def _split3(x):
    hi = x.astype(BF16)
    r1 = x - hi.astype(F32)
    mid = r1.astype(BF16)
    lo = (r1 - mid.astype(F32)).astype(BF16)
    return hi, mid, lo


def _mlstm_kernel(pm_ref, gc_ref, gr_ref, mg_ref, o_ref, c_ref, m_ref):
    R = pm_ref.shape[0]
    L = ML_L
    n_chunks = R // L
    NQK = 2 * N_MQK

    @pl.when(pl.program_id(0) == 0)
    def _():
        c_ref[...] = jnp.zeros_like(c_ref)
        m_ref[...] = jnp.zeros_like(m_ref)

    t_idx = lax.broadcasted_iota(jnp.int32, (L, L), 0)
    s_idx = lax.broadcasted_iota(jnp.int32, (L, L), 1)
    causal = s_idx <= t_idx
    tril = causal.astype(BF16)
    triu = (t_idx <= s_idx).astype(BF16)
    ones_ext = jnp.ones((L, ONES_W), BF16)

    H = range(M_HEADS)

    def rows_of(c):
        return slice(c * L, (c + 1) * L)

    for c0 in range(0, n_chunks, ML_GROUP):
        cs = range(c0, min(c0 + ML_GROUP, n_chunks))
        G = [(c, h) for c in cs for h in H]
        gr = {c: gr_ref[c] for c in cs}
        b_col_all = {c: sum(jnp.dot(tril, p, preferred_element_type=F32)
                            for p in _split3(gc_ref[rows_of(c), :])) for c in cs}
        b_row_all = {c: sum(jnp.dot(p, triu, preferred_element_type=F32)
                            for p in _split3(gr[c])) for c in cs}
        qh = {(c, h): pm_ref[rows_of(c), h * M_DQK:(h + 1) * M_DQK] for c, h in G}
        kh = {(c, h): pm_ref[rows_of(c), N_MQK + h * M_DQK:N_MQK + (h + 1) * M_DQK] for c, h in G}
        v_ext = {(c, h): jnp.concatenate(
            [pm_ref[rows_of(c), NQK + h * M_DV:NQK + (h + 1) * M_DV], ones_ext], axis=1) for c, h in G}
        i_row = {(c, h): gr[c][h:h + 1, :] for c, h in G}
        b_row = {(c, h): b_row_all[c][M_HEADS + h:M_HEADS + h + 1, :] for c, h in G}
        b_col = {(c, h): b_col_all[c][:, M_HEADS + h:M_HEADS + h + 1] for c, h in G}

        logd = {g: jnp.where(causal, b_col[g] - b_row[g] + i_row[g], -jnp.inf) for g in G}
        m_loc = {g: jnp.max(logd[g], axis=1, keepdims=True) for g in G}
        d_loc = {g: jnp.exp(logd[g] - m_loc[g]) for g in G}
        s = {g: lax.dot_general(qh[g], kh[g], (((1,), (1,)), ((), ())), preferred_element_type=F32) for g in G}
        sd = {g: (s[g] * d_loc[g]).astype(BF16) for g in G}
        p_ext = {g: jnp.dot(sd[g], v_ext[g], preferred_element_type=F32) for g in G}

        b_last = {g: b_row[g][:, L - 1:L] for g in G}
        w_log = {g: b_last[g] - b_row[g] + i_row[g] for g in G}
        a_loc = {g: jnp.max(w_log[g], axis=1, keepdims=True) for g in G}
        kw_t = {g: (kh[g].astype(F32).T * jnp.exp(w_log[g] - a_loc[g])).astype(BF16) for g in G}
        kv_ext = {g: jnp.dot(kw_t[g], v_ext[g], preferred_element_type=F32) for g in G}

        for c in cs:
            rows = rows_of(c)
            m_prev = [m_ref[h:h + 1, 0:1] for h in H]
            c_prev = [c_ref[h] for h in H]
            inter = [b_col[c, h] + m_prev[h] for h in H]
            m_t = [jnp.maximum(inter[h], m_loc[c, h]) for h in H]
            qc = [jnp.dot(qh[c, h], c_prev[h].astype(BF16), preferred_element_type=F32) for h in H]
            nd = [jnp.exp(inter[h] - m_t[h]) * qc[h] + jnp.exp(m_loc[c, h] - m_t[h]) * p_ext[c, h] for h in H]
            hh = [nd[h][:, :M_DV] / jnp.maximum(jnp.abs(nd[h][:, M_DV:M_DV + 1]), jnp.exp(-m_t[h])) for h in H]

            m_new = [jnp.maximum(b_last[c, h] + m_prev[h], a_loc[c, h]) for h in H]
            for h in H:
                c_ref[h] = (jnp.exp(b_last[c, h] + m_prev[h] - m_new[h]) * c_prev[h]
                            + jnp.exp(a_loc[c, h] - m_new[h]) * kv_ext[c, h])
                m_ref[h:h + 1, :] = jnp.broadcast_to(m_new[h], (1, m_ref.shape[1]))

            ms = [jnp.mean(hh[h] * hh[h], axis=-1, keepdims=True) for h in H]
            for h in H:
                cols = slice(h * M_DV, (h + 1) * M_DV)
                hn = (hh[h] * lax.rsqrt(ms[h] + EPS)) * mg_ref[:, cols]
                mo = pm_ref[rows, NQK + M_WIDTH + h * M_DV:NQK + M_WIDTH + (h + 1) * M_DV].astype(F32)
                o_ref[rows, cols] = (hn * jax.nn.sigmoid(mo)).astype(o_ref.dtype)


def _mlstm(pn, gc, gr, mnorm_g):
    S = pn.shape[0]
    R = ML_R
    wm = 2 * N_MQK + 2 * M_WIDTH
    return pl.pallas_call(
        _mlstm_kernel,
        grid=(S // R,),
        in_specs=[
            pl.BlockSpec((R, wm), lambda i: (i, 0)),
            pl.BlockSpec((R, 2 * M_HEADS), lambda i: (i, 0)),
            pl.BlockSpec((R // ML_L, 2 * M_HEADS, ML_L), lambda i: (i, 0, 0)),
            pl.BlockSpec((1, M_WIDTH), lambda i: (0, 0)),
        ],
        out_specs=pl.BlockSpec((R, M_WIDTH), lambda i: (i, 0)),
        out_shape=jax.ShapeDtypeStruct((S, M_WIDTH), BF16),
        scratch_shapes=[
            pltpu.VMEM((M_HEADS, M_DQK, M_DV + ONES_W), F32),
            pltpu.VMEM((8, LANES), F32),
        ],
        compiler_params=_cparams(("arbitrary",)),
        name="mlstm",
    )(pn, gc, gr, mnorm_g)


def _attn_kernel(qT_ref, k_ref, vT_ref, lam_ref, g_ref, o_ref,
                 qz_ref, p_ref, m_ref, l_ref, acc_ref, *, lam_init):
    qi = pl.program_id(1)
    tq, tk, blk = ATT_TQ, ATT_TK, ATT_BLK
    n_blocks = (qi + 1) * (tq // tk)
    ncol = 2 * tq
    wa = 256

    frow = lax.broadcasted_iota(jnp.int32, (2 * A_DH, blk), 0)
    for b in range(tq // blk):
        qb = qT_ref[b]
        zero = jnp.zeros_like(qb)
        qz_ref[:, b * blk:(b + 1) * blk] = jnp.where(frow < A_DH, qb, zero)
        qz_ref[:, tq + b * blk:tq + (b + 1) * blk] = jnp.where(frow >= A_DH, qb, zero)

    def key_block(t):
        return k_ref[pl.ds(pl.multiple_of(t * tk, tk), tk), :]

    def diag_visible(c, d):
        kc = (lax.broadcasted_iota(jnp.int32, (tk, wa), 0) + d * tk) // CHUNK
        qc = ((lax.broadcasted_iota(jnp.int32, (tk, wa), 1) + c * wa) % tq) // CHUNK
        return kc <= qc

    def scores(kb, c):
        return jnp.dot(kb, qz_ref[:, c * wa:(c + 1) * wa], preferred_element_type=F32)

    all_chunks = tuple(range(ncol // wa))
    n_diag = tq // tk
    first_diag = qi * n_diag

    def chunks_from(d):
        return tuple(c for c in all_chunks if (c * wa) % tq >= d * tk)

    k_first = k_ref[0:CHUNK, :]
    for c in all_chunks:
        cols = slice(c * wa, (c + 1) * wa)
        m_ref[:, cols] = jnp.max(scores(k_first, c), axis=0, keepdims=True)
    l_ref[...] = jnp.zeros_like(l_ref)
    acc_ref[...] = jnp.zeros_like(acc_ref)

    kb0 = key_block(0)
    for c in all_chunks:
        cols = slice(c * wa, (c + 1) * wa)
        vis = jnp.logical_or(diag_visible(c, 0), qi > 0)
        p = jnp.exp2(jnp.where(vis, scores(kb0, c), -jnp.inf) - m_ref[:, cols])
        l_ref[:, cols] += jnp.sum(p, axis=0, keepdims=True)
        p_ref[0, :, cols] = p.astype(BF16)

    def stage_ab(t, par, diag=None, chunks=all_chunks):
        kb = key_block(t)
        for c in chunks:
            cols = slice(c * wa, (c + 1) * wa)
            s = scores(kb, c)
            if diag is not None:
                s = jnp.where(diag_visible(c, diag), s, -jnp.inf)
            p = jnp.exp2(s - m_ref[:, cols])
            l_ref[:, cols] += jnp.sum(p, axis=0, keepdims=True)
            p_ref[par, :, cols] = p.astype(BF16)

    def stage_c(t, par, chunks=all_chunks):
        vb = vT_ref[t]
        for c in chunks:
            cols = slice(c * wa, (c + 1) * wa)
            acc_ref[:, cols] += jnp.dot(vb, p_ref[par, :, cols], preferred_element_type=F32)

    @pl.when(qi > 0)
    def _():
        def pair(t):
            stage_c(t, 0)
            stage_ab(t + 1, 1)
            stage_c(t + 1, 1)
            stage_ab(t + 2, 0)

        def two_pairs(v, carry):
            pair(4 * v)
            pair(4 * v + 2)
            return carry

        n_pairs = lax.shift_right_logical(first_diag, 1) - 1
        lax.fori_loop(0, lax.shift_right_logical(n_pairs, 1), two_pairs, 0)

        @pl.when(jnp.bitwise_and(n_pairs, 1) == 1)
        def _():
            pair(2 * (n_pairs - 1))

        stage_c(first_diag - 2, 0)
        stage_ab(first_diag - 1, 1)
        stage_c(first_diag - 1, 1)
        stage_ab(first_diag, 0, diag=0)

    for d in range(1, n_diag):
        stage_c(first_diag + d - 1, (d - 1) % 2, chunks=chunks_from(d - 1))
        stage_ab(first_diag + d, d % 2, diag=d, chunks=chunks_from(d))
    stage_c(first_diag + n_diag - 1, (n_diag - 1) % 2, chunks=chunks_from(n_diag - 1))

    l_fast = l_ref[...]
    overflow = jnp.max(jnp.where(l_fast < ATT_L_LIMIT, 0.0, 1.0)) > 0.5

    @pl.when(overflow)
    def _():
        m_ref[...] = jnp.full_like(m_ref, -jnp.inf)
        l_ref[...] = jnp.zeros_like(l_ref)
        acc_ref[...] = jnp.zeros_like(acc_ref)

        def block(t, carry):
            kb = key_block(t)
            vb = vT_ref[t]
            for c in range(ncol // wa):
                cols = slice(c * wa, (c + 1) * wa)
                kc = (lax.broadcasted_iota(jnp.int32, (tk, wa), 0) + t * tk) // CHUNK
                qc = ((lax.broadcasted_iota(jnp.int32, (tk, wa), 1) + c * wa) % tq + qi * tq) // CHUNK
                s = jnp.where(kc <= qc, scores(kb, c), -jnp.inf)
                m_old = m_ref[:, cols]
                m_new = jnp.maximum(m_old, jnp.max(s, axis=0, keepdims=True))
                alpha = jnp.exp2(m_old - m_new)
                p = jnp.exp2(s - m_new)
                l_ref[:, cols] = alpha * l_ref[:, cols] + jnp.sum(p, axis=0, keepdims=True)
                m_ref[:, cols] = m_new
                acc_ref[:, cols] = alpha * acc_ref[:, cols] + jnp.dot(
                    vb, p.astype(BF16), preferred_element_type=F32)
            return carry

        lax.fori_loop(0, n_blocks, block, 0)

    lam_p = lam_ref[...]
    lam = (jnp.exp(jnp.sum(lam_p[0:1] * lam_p[1:2], axis=1, keepdims=True))
           - jnp.exp(jnp.sum(lam_p[2:3] * lam_p[3:4], axis=1, keepdims=True)) + lam_init)
    acc = acc_ref[...]
    l = l_ref[...]
    out = acc[:, :tq] / l[:, :tq] - lam * (acc[:, tq:] / l[:, tq:])
    ms = jnp.mean(out * out, axis=0, keepdims=True)
    y = ((out * lax.rsqrt(ms + EPS)) * g_ref[...]) * (1.0 - lam_init)
    o_ref[...] = y.T.astype(o_ref.dtype)


def _attn(qvT, pn, lam_params, subln_col, lam_init):
    nblk = qvT.shape[0]
    S = pn.shape[0]
    tq, tk, blk = ATT_TQ, ATT_TK, ATT_BLK
    assert tk == blk and tq % (2 * tk) == 0
    k_col0 = AK_OFF // (2 * A_DH)
    return pl.pallas_call(
        functools.partial(_attn_kernel, lam_init=lam_init),
        grid=(A_HEADS, S // tq),
        in_specs=[
            pl.BlockSpec((tq // blk, 2 * A_DH, blk), lambda h, i: (i, h, 0)),
            pl.BlockSpec((S, 2 * A_DH), lambda h, i: (0, k_col0 + h)),
            pl.BlockSpec((nblk, A_DV, blk), lambda h, i: (0, A_HEADS + h, 0)),
            pl.BlockSpec((4, A_DH), lambda h, i: (0, 0)),
            pl.BlockSpec((A_DV, 1), lambda h, i: (0, 0)),
        ],
        out_specs=pl.BlockSpec((tq, A_DV), lambda h, i: (i, h)),
        out_shape=jax.ShapeDtypeStruct((S, A_WIDTH), BF16),
        scratch_shapes=[
            pltpu.VMEM((2 * A_DH, 2 * tq), BF16),
            pltpu.VMEM((2, tk, 2 * tq), BF16),
            pltpu.VMEM((1, 2 * tq), F32),
            pltpu.VMEM((1, 2 * tq), F32),
            pltpu.VMEM((A_DV, 2 * tq), F32),
        ],
        compiler_params=_cparams(("parallel", "arbitrary")),
        name="diff_attn",
    )(qvT, pn, qvT, lam_params, subln_col)


def _out_proj_kernel(hm_ref, ha_ref, w_ref, x_ref, g_ref, x1_ref, h2_ref):
    r = jnp.dot(hm_ref[...], w_ref[0:M_WIDTH, :], preferred_element_type=F32)
    r = r + jnp.dot(ha_ref[...], w_ref[M_WIDTH:, :], preferred_element_type=F32)
    x1 = x_ref[...] + r
    x1_ref[...] = x1
    ms = jnp.mean(x1 * x1, axis=-1, keepdims=True)
    h2_ref[...] = ((x1 * lax.rsqrt(ms + EPS)) * g_ref[...]).astype(h2_ref.dtype)


def _out_proj(hm, ha, w_out, x2d, g):
    S = x2d.shape[0]
    tm = OUT_TM
    return pl.pallas_call(
        _out_proj_kernel,
        grid=(S // tm,),
        in_specs=[
            pl.BlockSpec((tm, M_WIDTH), lambda i: (i, 0)),
            pl.BlockSpec((tm, A_WIDTH), lambda i: (i, 0)),
            pl.BlockSpec((M_WIDTH + A_WIDTH, D_MODEL), lambda i: (0, 0)),
            pl.BlockSpec((tm, D_MODEL), lambda i: (i, 0)),
            pl.BlockSpec((1, D_MODEL), lambda i: (0, 0)),
        ],
        out_specs=[
            pl.BlockSpec((tm, D_MODEL), lambda i: (i, 0)),
            pl.BlockSpec((tm, D_MODEL), lambda i: (i, 0)),
        ],
        out_shape=[
            jax.ShapeDtypeStruct((S, D_MODEL), F32),
            jax.ShapeDtypeStruct((S, D_MODEL), BF16),
        ],
        compiler_params=_cparams(("parallel",)),
        name="out_proj",
    )(hm, ha, w_out, x2d, g)


def _swiglu_kernel(h2_ref, wg_ref, wu_ref, wd_ref, x1_ref, fg_ref, o_ref, acc_ref):
    j = pl.program_id(1)

    @pl.when(j == 0)
    def _():
        acc_ref[...] = jnp.zeros_like(acc_ref)

    h2 = h2_ref[...]
    g = jnp.dot(h2, wg_ref[...], preferred_element_type=F32)
    u = jnp.dot(h2, wu_ref[...], preferred_element_type=F32)
    a = ((g * jax.nn.sigmoid(g)) * u).astype(BF16)
    acc_ref[...] += jnp.dot(a, wd_ref[...], preferred_element_type=F32)

    @pl.when(j == pl.num_programs(1) - 1)
    def _():
        y = x1_ref[...] + acc_ref[...]
        ms = jnp.mean(y * y, axis=-1, keepdims=True)
        o_ref[...] = (y * lax.rsqrt(ms + EPS)) * fg_ref[...]


def _swiglu(h2, wg, wu, wd, x1, fg):
    S = h2.shape[0]
    tm, tf = FF_TM, FF_TF
    return pl.pallas_call(
        _swiglu_kernel,
        grid=(S // tm, D_FF // tf),
        in_specs=[
            pl.BlockSpec((tm, D_MODEL), lambda i, j: (i, 0)),
            pl.BlockSpec((D_MODEL, tf), lambda i, j: (0, j)),
            pl.BlockSpec((D_MODEL, tf), lambda i, j: (0, j)),
            pl.BlockSpec((tf, D_MODEL), lambda i, j: (j, 0)),
            pl.BlockSpec((tm, D_MODEL), lambda i, j: (i, 0)),
            pl.BlockSpec((1, D_MODEL), lambda i, j: (0, 0)),
        ],
        out_specs=pl.BlockSpec((tm, D_MODEL), lambda i, j: (i, 0)),
        out_shape=jax.ShapeDtypeStruct((S, D_MODEL), F32),
        scratch_shapes=[pltpu.VMEM((tm, D_MODEL), F32)],
        compiler_params=_cparams(("parallel", "arbitrary")),
        name="swiglu",
    )(h2, wg, wu, wd, x1, fg)


O_MI = 2 * N_MQK + 2 * M_WIDTH
O_AQ = O_MI + 2 * M_HEADS
O_AK = O_AQ + A_WIDTH
O_AV = O_AK + A_WIDTH
N_IN = O_AV + A_WIDTH


def _wprep_kernel(w_ref, wn_ref, wt_ref, wg_ref):
    tr = w_ref.shape[0]
    wn_ref[:, 0:O_MI] = w_ref[:, 0:O_MI].astype(BF16)
    wn_ref[:, O_MI:N_NAT] = w_ref[:, O_AK:O_AV].astype(BF16)
    wg_ref[...] = jnp.concatenate(
        [w_ref[:, O_MI:O_AQ], jnp.zeros((tr, LANES - 2 * M_HEADS), F32)], axis=1).astype(BF16)
    wt_ref[0:A_WIDTH, :] = w_ref[:, O_AQ:O_AK].T.astype(BF16)
    wt_ref[A_WIDTH:2 * A_WIDTH, :] = w_ref[:, O_AV:N_IN].T.astype(BF16)


def _wprep(w):
    tr = WPREP_TR
    return pl.pallas_call(
        _wprep_kernel,
        grid=(D_MODEL // tr,),
        in_specs=[pl.BlockSpec((tr, N_IN), lambda i: (i, 0))],
        out_specs=[
            pl.BlockSpec((tr, N_NAT), lambda i: (i, 0)),
            pl.BlockSpec((2 * A_WIDTH, tr), lambda i: (0, i)),
            pl.BlockSpec((tr, LANES), lambda i: (i, 0)),
        ],
        out_shape=[
            jax.ShapeDtypeStruct((D_MODEL, N_NAT), BF16),
            jax.ShapeDtypeStruct((2 * A_WIDTH, D_MODEL), BF16),
            jax.ShapeDtypeStruct((D_MODEL, LANES), BF16),
        ],
        compiler_params=_cparams(("parallel",)),
        name="wprep",
    )(w)


def _rope_inv_freq():
    inv = ROPE_THETA ** (-jnp.arange(ROPE_HALF, dtype=F32) / ROPE_HALF)
    return jnp.tile(inv, LANES // ROPE_HALF).reshape(1, LANES), inv.reshape(ROPE_HALF, 1)


def kernel(x, norm1_g, w_in, conv_w, conv_b, b_igate, b_fgate, mnorm_g,
           lambda_q1, lambda_k1, lambda_q2, lambda_k2, subln_g, w_out,
           norm2_g, w_gate, w_up, w_down, final_g):
    B, S, D = x.shape
    assert (B, S, D) == (1, SEQ, D_MODEL) and norm1_g.shape[0] == 1
    l = 0
    x2d = x.reshape(S, D)

    assert w_in.shape[1:] == (D_MODEL, N_IN)
    w_nat, w_t, w_g = _wprep(w_in[l])
    b_g =jnp.pad(jnp.concatenate([b_igate[l], b_fgate[l]]), (0, LANES - 2 * M_HEADS)).reshape(1, LANES)

    inv_row, inv_col = _rope_inv_freq()

    gc, gr, pn, qvT = _front(x2d, norm1_g[l].reshape(1, D), w_g, b_g, w_nat, w_t,
                             inv_row, inv_col, conv_w[l], conv_b[l].reshape(1, -1))

    hm = _mlstm(pn, gc, gr, mnorm_g[l].reshape(1, M_WIDTH))

    lam_init = 0.8 - 0.6 * math.exp(-0.3 * l)
    lam_params = jnp.stack([lambda_q1[l], lambda_k1[l], lambda_q2[l], lambda_k2[l]]).astype(F32)
    ha = _attn(qvT, pn, lam_params, subln_g[l].reshape(A_DV, 1), lam_init)

    x1, h2 = _out_proj(hm, ha, w_out[l].astype(BF16), x2d, norm2_g[l].reshape(1, D))
    out = _swiglu(h2, w_gate[l].astype(BF16), w_up[l].astype(BF16), w_down[l].astype(BF16),
                  x1, final_g.reshape(1, D))
    return out.reshape(B, S, D)
```

```python
import functools
import math

import jax
import jax.numpy as jnp
from jax import lax
from jax.experimental import pallas as pl
from jax.experimental.pallas import tpu as pltpu

F32 = jnp.float32
BF16 = jnp.bfloat16

D_MODEL = 2048
SEQ = 16384
CHUNK = 64
EPS = 1e-6
ROPE_THETA = 10000.0

M_HEADS = 4
M_DV = 256
M_DQK = 128
M_CONV = 4
GATE_CAP = 15.0
M_WIDTH = M_HEADS * M_DV
N_MQK = M_HEADS * M_DQK

A_HEADS = 8
A_DH = 64
A_DV = 128
A_WIDTH = A_HEADS * A_DV
ROPE_HALF = A_DH // 2

D_FF = 5632

LANES = 128
VMEM_LIMIT = 56 * 1024 * 1024

FRONT_TM = 512
FRONT_TN = 512
ATT_BLK = 512
ATT_TQ = 2048
ATT_TK = 512
LOG2E = math.log2(math.e)
ATT_L_LIMIT = 2.0 ** 60
ML_L = 128
ML_R = 1024
WPREP_TR = 256
ML_GROUP = 1
OUT_TM = 512
FF_TM = 512
FF_TF = 512

N_NAT = 2 * N_MQK + 2 * M_WIDTH + A_WIDTH
AK_OFF = 2 * N_MQK + 2 * M_WIDTH
ONES_W = LANES


def _cparams(sem):
    return pltpu.CompilerParams(dimension_semantics=sem, vmem_limit_bytes=VMEM_LIMIT)


def _front_kernel(x_ref, g_ref, wg_ref, bg_ref, wn_ref, wt_ref, inv_row_ref, inv_col_ref, cw_ref, cb_ref,
                  gc_ref, gr_ref, pn_ref, qv_ref, h_ref, ybuf, tail_ref):
    tm = x_ref.shape[0]
    row0 = pl.program_id(0) * tm

    @pl.when(pl.program_id(0) == 0)
    def _():
        tail_ref[...] = jnp.zeros_like(tail_ref)

    x = x_ref[...]
    ms = jnp.mean(x * x, axis=-1, keepdims=True)
    h_ref[...] = ((x * lax.rsqrt(ms + EPS)) * g_ref[...]).astype(BF16)
    h = h_ref[...]

    pre = jnp.dot(h, wg_ref[...], preferred_element_type=F32) + bg_ref[...]
    capped = GATE_CAP * jnp.tanh(pre / GATE_CAP)
    logf = -(jnp.maximum(-capped, 0.0) + jnp.log1p(jnp.exp(-jnp.abs(capped))))
    lane = lax.broadcasted_iota(jnp.int32, pre.shape, 1)
    gates = jnp.where(lane < M_HEADS, capped, logf)
    gc_ref[...] = gates[:, :2 * M_HEADS]
    gates_t = gates.T[:2 * M_HEADS, :]
    for b in range(tm // LANES):
        gr_ref[b] = gates_t[:, b * LANES:(b + 1) * LANES]

    def row_major_tables():
        pos_c = (row0 + lax.broadcasted_iota(jnp.int32, (tm, LANES), 0)).astype(F32)
        ang = pos_c * inv_row_ref[...]
        ln = lax.broadcasted_iota(jnp.int32, (tm, LANES), 1)
        return jnp.cos(ang), jnp.where((ln % A_DH) < ROPE_HALF, -jnp.sin(ang), jnp.sin(ang))

    def feature_major_tables():
        pos_r = (row0 + lax.broadcasted_iota(jnp.int32, (ROPE_HALF, tm), 1)).astype(F32)
        ang_t = inv_col_ref[...] * pos_r
        return jnp.cos(ang_t), jnp.sin(ang_t)

    tn = FRONT_TN
    reps = tn // LANES
    cos_rm = sin_rm = None
    for j in range(N_NAT // tn):
        cols = slice(j * tn, (j + 1) * tn)
        r = jnp.dot(h, wn_ref[:, cols], preferred_element_type=F32)
        if (j + 1) * tn <= 2 * N_MQK:
            ybuf[0:8, :] = tail_ref[j]
            ybuf[8:8 + tm, :] = r
            tail_ref[j] = r[tm - 8:tm, :]
            y = cb_ref[:, cols]
            for tap in range(M_CONV - 1):
                off = 8 - (M_CONV - 1) + tap
                y = y + ybuf[off:off + tm, :] * cw_ref[tap:tap + 1, cols]
            y = y + r * cw_ref[M_CONV - 1:M_CONV, cols]
            r = y * jax.nn.sigmoid(y)
            if (j + 1) * tn <= N_MQK:
                r = r * (M_DQK ** -0.5)
        elif j * tn >= AK_OFF:
            if cos_rm is None:
                cos_rm, sin_rm = row_major_tables()
            c = jnp.concatenate([cos_rm] * reps, axis=1)
            s = jnp.concatenate([sin_rm] * reps, axis=1)
            first = (lax.broadcasted_iota(jnp.int32, r.shape, 1) % A_DH) < ROPE_HALF
            swapped = jnp.where(first, pltpu.roll(r, tn - ROPE_HALF, 1), pltpu.roll(r, ROPE_HALF, 1))
            r = r * c + swapped * s
        pn_ref[:, cols] = r.astype(pn_ref.dtype)

    rows = 256
    scale = A_DH ** -0.5 * LOG2E
    cT, sT = feature_major_tables()
    for rb in range(wt_ref.shape[0] // rows):
        r = lax.dot_general(wt_ref[rb * rows:(rb + 1) * rows, :], h,
                            (((1,), (1,)), ((), ())), preferred_element_type=F32)
        if rb * rows < A_WIDTH:
            parts = []
            for g in range(rows // A_DH):
                t1 = r[g * A_DH:g * A_DH + ROPE_HALF]
                t2 = r[g * A_DH + ROPE_HALF:(g + 1) * A_DH]
                parts.append((t1 * cT - t2 * sT) * scale)
                parts.append((t2 * cT + t1 * sT) * scale)
            r = jnp.concatenate(parts, axis=0)
        qv_ref[0, rb * rows:(rb + 1) * rows, :] = r.astype(qv_ref.dtype)


def _front(x2d, g, wg, bg, w_nat, w_t, inv_row, inv_col, conv_w, conv_b):
    S = x2d.shape[0]
    tm = FRONT_TM
    assert tm == ATT_BLK and FRONT_TN == N_MQK
    nfeat = w_t.shape[0]
    const = lambda i: (0, 0)
    resident = dict(pipeline_mode=pl.Buffered(1))
    return pl.pallas_call(
        _front_kernel,
        grid=(S // tm,),
        in_specs=[
            pl.BlockSpec((tm, D_MODEL), lambda i: (i, 0)),
            pl.BlockSpec((1, D_MODEL), const),
            pl.BlockSpec((D_MODEL, LANES), const),
            pl.BlockSpec((1, LANES), const),
            pl.BlockSpec((D_MODEL, N_NAT), const, **resident),
            pl.BlockSpec((nfeat, D_MODEL), const, **resident),
            pl.BlockSpec((1, LANES), const),
            pl.BlockSpec((ROPE_HALF, 1), const),
            pl.BlockSpec((M_CONV, 2 * N_MQK), const),
            pl.BlockSpec((1, 2 * N_MQK), const),
        ],
        out_specs=[
            pl.BlockSpec((tm, 2 * M_HEADS), lambda i: (i, 0)),
            pl.BlockSpec((tm // LANES, 2 * M_HEADS, LANES), lambda i: (i, 0, 0)),
            pl.BlockSpec((tm, N_NAT), lambda i: (i, 0)),
            pl.BlockSpec((1, nfeat, ATT_BLK), lambda i: (i, 0, 0)),
        ],
        out_shape=[
            jax.ShapeDtypeStruct((S, 2 * M_HEADS), F32),
            jax.ShapeDtypeStruct((S // LANES, 2 * M_HEADS, LANES), F32),
            jax.ShapeDtypeStruct((S, N_NAT), BF16),
            jax.ShapeDtypeStruct((S // ATT_BLK, nfeat, ATT_BLK), BF16),
        ],
        scratch_shapes=[
            pltpu.VMEM((tm, D_MODEL), BF16),
            pltpu.VMEM((tm + 8, FRONT_TN), F32),
            pltpu.VMEM((2 * N_MQK // FRONT_TN, 8, FRONT_TN), F32),
        ],
        compiler_params=_cparams(("arbitrary",)),
        name="front",
    )(x2d, g, wg, bg, w_nat, w_t, inv_row, inv_col, conv_w, conv_b)


def _split3(x):
    hi = x.astype(BF16)
    r1 = x - hi.astype(F32)
    mid = r1.astype(BF16)
    lo = (r1 - mid.astype(F32)).astype(BF16)
    return hi, mid, lo


def _mlstm_kernel(pm_ref, gc_ref, gr_ref, mg_ref, o_ref, c_ref, m_ref):
    R = pm_ref.shape[0]
    L = ML_L
    n_chunks = R // L
    NQK = 2 * N_MQK

    @pl.when(pl.program_id(0) == 0)
    def _():
        c_ref[...] = jnp.zeros_like(c_ref)
        m_ref[...] = jnp.zeros_like(m_ref)

    t_idx = lax.broadcasted_iota(jnp.int32, (L, L), 0)
    s_idx = lax.broadcasted_iota(jnp.int32, (L, L), 1)
    causal = s_idx <= t_idx
    tril = causal.astype(BF16)
    triu = (t_idx <= s_idx).astype(BF16)
    ones_ext = jnp.ones((L, ONES_W), BF16)

    H = range(M_HEADS)

    def rows_of(c):
        return slice(c * L, (c + 1) * L)

    for c0 in range(0, n_chunks, ML_GROUP):
        cs = range(c0, min(c0 + ML_GROUP, n_chunks))
        G = [(c, h) for c in cs for h in H]
        gr = {c: gr_ref[c] for c in cs}
        b_col_all = {c: sum(jnp.dot(tril, p, preferred_element_type=F32)
                            for p in _split3(gc_ref[rows_of(c), :])) for c in cs}
        b_row_all = {c: sum(jnp.dot(p, triu, preferred_element_type=F32)
                            for p in _split3(gr[c])) for c in cs}
        qh = {(c, h): pm_ref[rows_of(c), h * M_DQK:(h + 1) * M_DQK] for c, h in G}
        kh = {(c, h): pm_ref[rows_of(c), N_MQK + h * M_DQK:N_MQK + (h + 1) * M_DQK] for c, h in G}
        v_ext = {(c, h): jnp.concatenate(
            [pm_ref[rows_of(c), NQK + h * M_DV:NQK + (h + 1) * M_DV], ones_ext], axis=1) for c, h in G}
        i_row = {(c, h): gr[c][h:h + 1, :] for c, h in G}
        b_row = {(c, h): b_row_all[c][M_HEADS + h:M_HEADS + h + 1, :] for c, h in G}
        b_col = {(c, h): b_col_all[c][:, M_HEADS + h:M_HEADS + h + 1] for c, h in G}

        logd = {g: jnp.where(causal, b_col[g] - b_row[g] + i_row[g], -jnp.inf) for g in G}
        m_loc = {g: jnp.max(logd[g], axis=1, keepdims=True) for g in G}
        d_loc = {g: jnp.exp(logd[g] - m_loc[g]) for g in G}
        s = {g: lax.dot_general(qh[g], kh[g], (((1,), (1,)), ((), ())), preferred_element_type=F32) for g in G}
        sd = {g: (s[g] * d_loc[g]).astype(BF16) for g in G}
        p_ext = {g: jnp.dot(sd[g], v_ext[g], preferred_element_type=F32) for g in G}

        b_last = {g: b_row[g][:, L - 1:L] for g in G}
        w_log = {g: b_last[g] - b_row[g] + i_row[g] for g in G}
        a_loc = {g: jnp.max(w_log[g], axis=1, keepdims=True) for g in G}
        kw_t = {g: (kh[g].astype(F32).T * jnp.exp(w_log[g] - a_loc[g])).astype(BF16) for g in G}
        kv_ext = {g: jnp.dot(kw_t[g], v_ext[g], preferred_element_type=F32) for g in G}

        for c in cs:
            rows = rows_of(c)
            m_prev = [m_ref[h:h + 1, 0:1] for h in H]
            c_prev = [c_ref[h] for h in H]
            inter = [b_col[c, h] + m_prev[h] for h in H]
            m_t = [jnp.maximum(inter[h], m_loc[c, h]) for h in H]
            qc = [jnp.dot(qh[c, h], c_prev[h].astype(BF16), preferred_element_type=F32) for h in H]
            nd = [jnp.exp(inter[h] - m_t[h]) * qc[h] + jnp.exp(m_loc[c, h] - m_t[h]) * p_ext[c, h] for h in H]
            hh = [nd[h][:, :M_DV] / jnp.maximum(jnp.abs(nd[h][:, M_DV:M_DV + 1]), jnp.exp(-m_t[h])) for h in H]

            m_new = [jnp.maximum(b_last[c, h] + m_prev[h], a_loc[c, h]) for h in H]
            for h in H:
                c_ref[h] = (jnp.exp(b_last[c, h] + m_prev[h] - m_new[h]) * c_prev[h]
                            + jnp.exp(a_loc[c, h] - m_new[h]) * kv_ext[c, h])
                m_ref[h:h + 1, :] = jnp.broadcast_to(m_new[h], (1, m_ref.shape[1]))

            ms = [jnp.mean(hh[h] * hh[h], axis=-1, keepdims=True) for h in H]
            for h in H:
                cols = slice(h * M_DV, (h + 1) * M_DV)
                hn = (hh[h] * lax.rsqrt(ms[h] + EPS)) * mg_ref[:, cols]
                mo = pm_ref[rows, NQK + M_WIDTH + h * M_DV:NQK + M_WIDTH + (h + 1) * M_DV].astype(F32)
                o_ref[rows, cols] = (hn * jax.nn.sigmoid(mo)).astype(o_ref.dtype)


def _mlstm(pn, gc, gr, mnorm_g):
    S = pn.shape[0]
    R = ML_R
    wm = 2 * N_MQK + 2 * M_WIDTH
    return pl.pallas_call(
        _mlstm_kernel,
        grid=(S // R,),
        in_specs=[
            pl.BlockSpec((R, wm), lambda i: (i, 0)),
            pl.BlockSpec((R, 2 * M_HEADS), lambda i: (i, 0)),
            pl.BlockSpec((R // ML_L, 2 * M_HEADS, ML_L), lambda i: (i, 0, 0)),
            pl.BlockSpec((1, M_WIDTH), lambda i: (0, 0)),
        ],
        out_specs=pl.BlockSpec((R, M_WIDTH), lambda i: (i, 0)),
        out_shape=jax.ShapeDtypeStruct((S, M_WIDTH), BF16),
        scratch_shapes=[
            pltpu.VMEM((M_HEADS, M_DQK, M_DV + ONES_W), F32),
            pltpu.VMEM((8, LANES), F32),
        ],
        compiler_params=_cparams(("arbitrary",)),
        name="mlstm",
    )(pn, gc, gr, mnorm_g)


def _attn_kernel(qT_ref, k_ref, vT_ref, lam_ref, g_ref, o_ref,
                 qz_ref, p_ref, m_ref, l_ref, acc_ref, *, lam_init):
    qi = pl.program_id(1)
    tq, tk, blk = ATT_TQ, ATT_TK, ATT_BLK
    n_blocks = (qi + 1) * (tq // tk)
    ncol = 2 * tq
    wa = 256

    frow = lax.broadcasted_iota(jnp.int32, (2 * A_DH, blk), 0)
    for b in range(tq // blk):
        qb = qT_ref[b]
        zero = jnp.zeros_like(qb)
        qz_ref[:, b * blk:(b + 1) * blk] = jnp.where(frow < A_DH, qb, zero)
        qz_ref[:, tq + b * blk:tq + (b + 1) * blk] = jnp.where(frow >= A_DH, qb, zero)

    def key_block(t):
        return k_ref[pl.ds(pl.multiple_of(t * tk, tk), tk), :]

    def diag_visible(c, d):
        kc = (lax.broadcasted_iota(jnp.int32, (tk, wa), 0) + d * tk) // CHUNK
        qc = ((lax.broadcasted_iota(jnp.int32, (tk, wa), 1) + c * wa) % tq) // CHUNK
        return kc <= qc

    def scores(kb, c):
        return jnp.dot(kb, qz_ref[:, c * wa:(c + 1) * wa], preferred_element_type=F32)

    all_chunks = tuple(range(ncol // wa))
    n_diag = tq // tk
    first_diag = qi * n_diag

    def chunks_from(d):
        return tuple(c for c in all_chunks if (c * wa) % tq >= d * tk)

    k_first = k_ref[0:CHUNK, :]
    for c in all_chunks:
        cols = slice(c * wa, (c + 1) * wa)
        m_ref[:, cols] = jnp.max(scores(k_first, c), axis=0, keepdims=True)
    l_ref[...] = jnp.zeros_like(l_ref)
    acc_ref[...] = jnp.zeros_like(acc_ref)

    kb0 = key_block(0)
    for c in all_chunks:
        cols = slice(c * wa, (c + 1) * wa)
        vis = jnp.logical_or(diag_visible(c, 0), qi > 0)
        p = jnp.exp2(jnp.where(vis, scores(kb0, c), -jnp.inf) - m_ref[:, cols])
        l_ref[:, cols] += jnp.sum(p, axis=0, keepdims=True)
        p_ref[0, :, cols] = p.astype(BF16)

    def stage_ab(t, par, diag=None, chunks=all_chunks):
        kb = key_block(t)
        for c in chunks:
            cols = slice(c * wa, (c + 1) * wa)
            s = scores(kb, c)
            if diag is not None:
                s = jnp.where(diag_visible(c, diag), s, -jnp.inf)
            p = jnp.exp2(s - m_ref[:, cols])
            l_ref[:, cols] += jnp.sum(p, axis=0, keepdims=True)
            p_ref[par, :, cols] = p.astype(BF16)

    def stage_c(t, par, chunks=all_chunks):
        vb = vT_ref[t]
        for c in chunks:
            cols = slice(c * wa, (c + 1) * wa)
            acc_ref[:, cols] += jnp.dot(vb, p_ref[par, :, cols], preferred_element_type=F32)

    @pl.when(qi > 0)
    def _():
        def pair(t):
            stage_c(t, 0)
            stage_ab(t + 1, 1)
            stage_c(t + 1, 1)
            stage_ab(t + 2, 0)

        def two_pairs(v, carry):
            pair(4 * v)
            pair(4 * v + 2)
            return carry

        n_pairs = lax.shift_right_logical(first_diag, 1) - 1
        lax.fori_loop(0, lax.shift_right_logical(n_pairs, 1), two_pairs, 0)

        @pl.when(jnp.bitwise_and(n_pairs, 1) == 1)
        def _():
            pair(2 * (n_pairs - 1))

        stage_c(first_diag - 2, 0)
        stage_ab(first_diag - 1, 1)
        stage_c(first_diag - 1, 1)
        stage_ab(first_diag, 0, diag=0)

    for d in range(1, n_diag):
        stage_c(first_diag + d - 1, (d - 1) % 2, chunks=chunks_from(d - 1))
        stage_ab(first_diag + d, d % 2, diag=d, chunks=chunks_from(d))
    stage_c(first_diag + n_diag - 1, (n_diag - 1) % 2, chunks=chunks_from(n_diag - 1))

    l_fast = l_ref[...]
    overflow = jnp.max(jnp.where(l_fast < ATT_L_LIMIT, 0.0, 1.0)) > 0.5

    @pl.when(overflow)
    def _():
        m_ref[...] = jnp.full_like(m_ref, -jnp.inf)
        l_ref[...] = jnp.zeros_like(l_ref)
        acc_ref[...] = jnp.zeros_like(acc_ref)

        def block(t, carry):
            kb = key_block(t)
            vb = vT_ref[t]
            for c in range(ncol // wa):
                cols = slice(c * wa, (c + 1) * wa)
                kc = (lax.broadcasted_iota(jnp.int32, (tk, wa), 0) + t * tk) // CHUNK
                qc = ((lax.broadcasted_iota(jnp.int32, (tk, wa), 1) + c * wa) % tq + qi * tq) // CHUNK
                s = jnp.where(kc <= qc, scores(kb, c), -jnp.inf)
                m_old = m_ref[:, cols]
                m_new = jnp.maximum(m_old, jnp.max(s, axis=0, keepdims=True))
                alpha = jnp.exp2(m_old - m_new)
                p = jnp.exp2(s - m_new)
                l_ref[:, cols] = alpha * l_ref[:, cols] + jnp.sum(p, axis=0, keepdims=True)
                m_ref[:, cols] = m_new
                acc_ref[:, cols] = alpha * acc_ref[:, cols] + jnp.dot(
                    vb, p.astype(BF16), preferred_element_type=F32)
            return carry

        lax.fori_loop(0, n_blocks, block, 0)

    lam_p = lam_ref[...]
    lam = (jnp.exp(jnp.sum(lam_p[0:1] * lam_p[1:2], axis=1, keepdims=True))
           - jnp.exp(jnp.sum(lam_p[2:3] * lam_p[3:4], axis=1, keepdims=True)) + lam_init)
    acc = acc_ref[...]
    l = l_ref[...]
    out = acc[:, :tq] / l[:, :tq] - lam * (acc[:, tq:] / l[:, tq:])
    ms = jnp.mean(out * out, axis=0, keepdims=True)
    y = ((out * lax.rsqrt(ms + EPS)) * g_ref[...]) * (1.0 - lam_init)
    o_ref[...] = y.T.astype(o_ref.dtype)


def _attn(qvT, pn, lam_params, subln_col, lam_init):
    nblk = qvT.shape[0]
    S = pn.shape[0]
    tq, tk, blk = ATT_TQ, ATT_TK, ATT_BLK
    assert tk == blk and tq % (2 * tk) == 0
    k_col0 = AK_OFF // (2 * A_DH)
    return pl.pallas_call(
        functools.partial(_attn_kernel, lam_init=lam_init),
        grid=(A_HEADS, S // tq),
        in_specs=[
            pl.BlockSpec((tq // blk, 2 * A_DH, blk), lambda h, i: (i, h, 0)),
            pl.BlockSpec((S, 2 * A_DH), lambda h, i: (0, k_col0 + h)),
            pl.BlockSpec((nblk, A_DV, blk), lambda h, i: (0, A_HEADS + h, 0)),
            pl.BlockSpec((4, A_DH), lambda h, i: (0, 0)),
            pl.BlockSpec((A_DV, 1), lambda h, i: (0, 0)),
        ],
        out_specs=pl.BlockSpec((tq, A_DV), lambda h, i: (i, h)),
        out_shape=jax.ShapeDtypeStruct((S, A_WIDTH), BF16),
        scratch_shapes=[
            pltpu.VMEM((2 * A_DH, 2 * tq), BF16),
            pltpu.VMEM((2, tk, 2 * tq), BF16),
            pltpu.VMEM((1, 2 * tq), F32),
            pltpu.VMEM((1, 2 * tq), F32),
            pltpu.VMEM((A_DV, 2 * tq), F32),
        ],
        compiler_params=_cparams(("parallel", "arbitrary")),
        name="diff_attn",
    )(qvT, pn, qvT, lam_params, subln_col)


def _out_proj_kernel(hm_ref, ha_ref, w_ref, x_ref, g_ref, x1_ref, h2_ref):
    r = jnp.dot(hm_ref[...], w_ref[0:M_WIDTH, :], preferred_element_type=F32)
    r = r + jnp.dot(ha_ref[...], w_ref[M_WIDTH:, :], preferred_element_type=F32)
    x1 = x_ref[...] + r
    x1_ref[...] = x1
    ms = jnp.mean(x1 * x1, axis=-1, keepdims=True)
    h2_ref[...] = ((x1 * lax.rsqrt(ms + EPS)) * g_ref[...]).astype(h2_ref.dtype)


def _out_proj(hm, ha, w_out, x2d, g):
    S = x2d.shape[0]
    tm = OUT_TM
    return pl.pallas_call(
        _out_proj_kernel,
        grid=(S // tm,),
        in_specs=[
            pl.BlockSpec((tm, M_WIDTH), lambda i: (i, 0)),
            pl.BlockSpec((tm, A_WIDTH), lambda i: (i, 0)),
            pl.BlockSpec((M_WIDTH + A_WIDTH, D_MODEL), lambda i: (0, 0)),
            pl.BlockSpec((tm, D_MODEL), lambda i: (i, 0)),
            pl.BlockSpec((1, D_MODEL), lambda i: (0, 0)),
        ],
        out_specs=[
            pl.BlockSpec((tm, D_MODEL), lambda i: (i, 0)),
            pl.BlockSpec((tm, D_MODEL), lambda i: (i, 0)),
        ],
        out_shape=[
            jax.ShapeDtypeStruct((S, D_MODEL), F32),
            jax.ShapeDtypeStruct((S, D_MODEL), BF16),
        ],
        compiler_params=_cparams(("parallel",)),
        name="out_proj",
    )(hm, ha, w_out, x2d, g)


def _swiglu_kernel(h2_ref, wg_ref, wu_ref, wd_ref, x1_ref, fg_ref, o_ref, acc_ref):
    j = pl.program_id(1)

    @pl.when(j == 0)
    def _():
        acc_ref[...] = jnp.zeros_like(acc_ref)

    h2 = h2_ref[...]
    g = jnp.dot(h2, wg_ref[...], preferred_element_type=F32)
    u = jnp.dot(h2, wu_ref[...], preferred_element_type=F32)
    a = ((g * jax.nn.sigmoid(g)) * u).astype(BF16)
    acc_ref[...] += jnp.dot(a, wd_ref[...], preferred_element_type=F32)

    @pl.when(j == pl.num_programs(1) - 1)
    def _():
        y = x1_ref[...] + acc_ref[...]
        ms = jnp.mean(y * y, axis=-1, keepdims=True)
        o_ref[...] = (y * lax.rsqrt(ms + EPS)) * fg_ref[...]


def _swiglu(h2, wg, wu, wd, x1, fg):
    S = h2.shape[0]
    tm, tf = FF_TM, FF_TF
    return pl.pallas_call(
        _swiglu_kernel,
        grid=(S // tm, D_FF // tf),
        in_specs=[
            pl.BlockSpec((tm, D_MODEL), lambda i, j: (i, 0)),
            pl.BlockSpec((D_MODEL, tf), lambda i, j: (0, j)),
            pl.BlockSpec((D_MODEL, tf), lambda i, j: (0, j)),
            pl.BlockSpec((tf, D_MODEL), lambda i, j: (j, 0)),
            pl.BlockSpec((tm, D_MODEL), lambda i, j: (i, 0)),
            pl.BlockSpec((1, D_MODEL), lambda i, j: (0, 0)),
        ],
        out_specs=pl.BlockSpec((tm, D_MODEL), lambda i, j: (i, 0)),
        out_shape=jax.ShapeDtypeStruct((S, D_MODEL), F32),
        scratch_shapes=[pltpu.VMEM((tm, D_MODEL), F32)],
        compiler_params=_cparams(("parallel", "arbitrary")),
        name="swiglu",
    )(h2, wg, wu, wd, x1, fg)


O_MI = 2 * N_MQK + 2 * M_WIDTH
O_AQ = O_MI + 2 * M_HEADS
O_AK = O_AQ + A_WIDTH
O_AV = O_AK + A_WIDTH
N_IN = O_AV + A_WIDTH


def _wprep_kernel(w_ref, wn_ref, wt_ref, wg_ref):
    tr = w_ref.shape[0]
    wn_ref[:, 0:O_MI] = w_ref[:, 0:O_MI].astype(BF16)
    wn_ref[:, O_MI:N_NAT] = w_ref[:, O_AK:O_AV].astype(BF16)
    wg_ref[...] = jnp.concatenate(
        [w_ref[:, O_MI:O_AQ], jnp.zeros((tr, LANES - 2 * M_HEADS), F32)], axis=1).astype(BF16)
    wt_ref[0:A_WIDTH, :] = w_ref[:, O_AQ:O_AK].T.astype(BF16)
    wt_ref[A_WIDTH:2 * A_WIDTH, :] = w_ref[:, O_AV:N_IN].T.astype(BF16)


def _wprep(w):
    tr = WPREP_TR
    return pl.pallas_call(
        _wprep_kernel,
        grid=(D_MODEL // tr,),
        in_specs=[pl.BlockSpec((tr, N_IN), lambda i: (i, 0))],
        out_specs=[
            pl.BlockSpec((tr, N_NAT), lambda i: (i, 0)),
            pl.BlockSpec((2 * A_WIDTH, tr), lambda i: (0, i)),
            pl.BlockSpec((tr, LANES), lambda i: (i, 0)),
        ],
        out_shape=[
            jax.ShapeDtypeStruct((D_MODEL, N_NAT), BF16),
            jax.ShapeDtypeStruct((2 * A_WIDTH, D_MODEL), BF16),
            jax.ShapeDtypeStruct((D_MODEL, LANES), BF16),
        ],
        compiler_params=_cparams(("parallel",)),
        name="wprep",
    )(w)


def _rope_inv_freq():
    inv = ROPE_THETA ** (-jnp.arange(ROPE_HALF, dtype=F32) / ROPE_HALF)
    return jnp.tile(inv, LANES // ROPE_HALF).reshape(1, LANES), inv.reshape(ROPE_HALF, 1)


def kernel(x, norm1_g, w_in, conv_w, conv_b, b_igate, b_fgate, mnorm_g,
           lambda_q1, lambda_k1, lambda_q2, lambda_k2, subln_g, w_out,
           norm2_g, w_gate, w_up, w_down, final_g):
    B, S, D = x.shape
    assert (B, S, D) == (1, SEQ, D_MODEL) and norm1_g.shape[0] == 1
    l = 0
    x2d = x.reshape(S, D)

    assert w_in.shape == (1, D_MODEL, N_IN)
    w_nat, w_t, w_g = _wprep(w_in.reshape(D_MODEL, N_IN))
    b_g =jnp.pad(jnp.concatenate([b_igate[l], b_fgate[l]]), (0, LANES - 2 * M_HEADS)).reshape(1, LANES)

    inv_row, inv_col = _rope_inv_freq()

    gc, gr, pn, qvT = _front(x2d, norm1_g[l].reshape(1, D), w_g, b_g, w_nat, w_t,
                             inv_row, inv_col, conv_w[l], conv_b[l].reshape(1, -1))

    hm = _mlstm(pn, gc, gr, mnorm_g[l].reshape(1, M_WIDTH))

    lam_init = 0.8 - 0.6 * math.exp(-0.3 * l)
    lam_params = jnp.stack([lambda_q1[l], lambda_k1[l], lambda_q2[l], lambda_k2[l]]).astype(F32)
    ha = _attn(qvT, pn, lam_params, subln_g[l].reshape(A_DV, 1), lam_init)

    x1, h2 = _out_proj(hm, ha, w_out[l].astype(BF16), x2d, norm2_g[l].reshape(1, D))
    out = _swiglu(h2, w_gate[l].astype(BF16), w_up[l].astype(BF16), w_down[l].astype(BF16),
                  x1, final_g.reshape(1, D))
    return out.reshape(B, S, D)
```

```python
import functools
import math

import jax
import jax.numpy as jnp
from jax import lax
from jax.experimental import pallas as pl
from jax.experimental.pallas import tpu as pltpu

F32 = jnp.float32
BF16 = jnp.bfloat16

D_MODEL = 2048
SEQ = 16384
CHUNK = 64
EPS = 1e-6
ROPE_THETA = 10000.0

M_HEADS = 4
M_DV = 256
M_DQK = 128
M_CONV = 4
GATE_CAP = 15.0
M_WIDTH = M_HEADS * M_DV
N_MQK = M_HEADS * M_DQK

A_HEADS = 8
A_DH = 64
A_DV = 128
A_WIDTH = A_HEADS * A_DV
ROPE_HALF = A_DH // 2

D_FF = 5632

LANES = 128
VMEM_LIMIT = 56 * 1024 * 1024

FRONT_TM = 512
FRONT_TN = 512
ATT_BLK = 512
ATT_TQ = 2048
ATT_TK = 512
LOG2E = math.log2(math.e)
ATT_L_LIMIT = 2.0 ** 60
ML_L = 128
ML_R = 1024
WPREP_TR = 256
ML_GROUP = 1
OUT_TM = 512
FF_TM = 512
FF_TF = 512

N_NAT = 2 * N_MQK + 2 * M_WIDTH + A_WIDTH
AK_OFF = 2 * N_MQK + 2 * M_WIDTH
ONES_W = LANES


def _cparams(sem):
    return pltpu.CompilerParams(dimension_semantics=sem, vmem_limit_bytes=VMEM_LIMIT)


def _front_kernel(x_ref, g_ref, wg_ref, bg_ref, wn_ref, wt_ref, inv_row_ref, inv_col_ref, cw_ref, cb_ref,
                  gc_ref, gr_ref, pn_ref, qv_ref, h_ref, ybuf, tail_ref):
    tm = x_ref.shape[0]
    row0 = pl.program_id(0) * tm

    @pl.when(pl.program_id(0) == 0)
    def _():
        tail_ref[...] = jnp.zeros_like(tail_ref)

    x = x_ref[...]
    ms = jnp.mean(x * x, axis=-1, keepdims=True)
    h_ref[...] = ((x * lax.rsqrt(ms + EPS)) * g_ref[...]).astype(BF16)
    h = h_ref[...]

    pre = jnp.dot(h, wg_ref[...], preferred_element_type=F32) + bg_ref[...]
    capped = GATE_CAP * jnp.tanh(pre / GATE_CAP)
    logf = -(jnp.maximum(-capped, 0.0) + jnp.log1p(jnp.exp(-jnp.abs(capped))))
    lane = lax.broadcasted_iota(jnp.int32, pre.shape, 1)
    gates = jnp.where(lane < M_HEADS, capped, logf)
    gc_ref[...] = gates[:, :2 * M_HEADS]
    gates_t = gates.T[:2 * M_HEADS, :]
    for b in range(tm // LANES):
        gr_ref[b] = gates_t[:, b * LANES:(b + 1) * LANES]

    def row_major_tables():
        pos_c = (row0 + lax.broadcasted_iota(jnp.int32, (tm, LANES), 0)).astype(F32)
        ang = pos_c * inv_row_ref[...]
        ln = lax.broadcasted_iota(jnp.int32, (tm, LANES), 1)
        return jnp.cos(ang), jnp.where((ln % A_DH) < ROPE_HALF, -jnp.sin(ang), jnp.sin(ang))

    def feature_major_tables():
        pos_r = (row0 + lax.broadcasted_iota(jnp.int32, (ROPE_HALF, tm), 1)).astype(F32)
        ang_t = inv_col_ref[...] * pos_r
        return jnp.cos(ang_t), jnp.sin(ang_t)

    tn = FRONT_TN
    reps = tn // LANES
    cos_rm = sin_rm = None
    for j in range(N_NAT // tn):
        cols = slice(j * tn, (j + 1) * tn)
        r = jnp.dot(h, wn_ref[:, cols], preferred_element_type=F32)
        if (j + 1) * tn <= 2 * N_MQK:
            ybuf[0:8, :] = tail_ref[j]
            ybuf[8:8 + tm, :] = r
            tail_ref[j] = r[tm - 8:tm, :]
            y = cb_ref[:, cols]
            for tap in range(M_CONV - 1):
                off = 8 - (M_CONV - 1) + tap
                y = y + ybuf[off:off + tm, :] * cw_ref[tap:tap + 1, cols]
            y = y + r * cw_ref[M_CONV - 1:M_CONV, cols]
            r = y * jax.nn.sigmoid(y)
            if (j + 1) * tn <= N_MQK:
                r = r * (M_DQK ** -0.5)
        elif j * tn >= AK_OFF:
            if cos_rm is None:
                cos_rm, sin_rm = row_major_tables()
            c = jnp.concatenate([cos_rm] * reps, axis=1)
            s = jnp.concatenate([sin_rm] * reps, axis=1)
            first = (lax.broadcasted_iota(jnp.int32, r.shape, 1) % A_DH) < ROPE_HALF
            swapped = jnp.where(first, pltpu.roll(r, tn - ROPE_HALF, 1), pltpu.roll(r, ROPE_HALF, 1))
            r = r * c + swapped * s
        pn_ref[:, cols] = r.astype(pn_ref.dtype)

    rows = 256
    scale = A_DH ** -0.5 * LOG2E
    cT, sT = feature_major_tables()
    for rb in range(wt_ref.shape[0] // rows):
        r = lax.dot_general(wt_ref[rb * rows:(rb + 1) * rows, :], h,
                            (((1,), (1,)), ((), ())), preferred_element_type=F32)
        if rb * rows < A_WIDTH:
            parts = []
            for g in range(rows // A_DH):
                t1 = r[g * A_DH:g * A_DH + ROPE_HALF]
                t2 = r[g * A_DH + ROPE_HALF:(g + 1) * A_DH]
                parts.append((t1 * cT - t2 * sT) * scale)
                parts.append((t2 * cT + t1 * sT) * scale)
            r = jnp.concatenate(parts, axis=0)
        qv_ref[0, rb * rows:(rb + 1) * rows, :] = r.astype(qv_ref.dtype)


def _front(x2d, g, wg, bg, w_nat, w_t, inv_row, inv_col, conv_w, conv_b):
    S = x2d.shape[0]
    tm = FRONT_TM
    assert tm == ATT_BLK and FRONT_TN == N_MQK
    nfeat = w_t.shape[0]
    const = lambda i: (0, 0)
    resident = dict(pipeline_mode=pl.Buffered(1))
    return pl.pallas_call(
        _front_kernel,
        grid=(S // tm,),
        in_specs=[
            pl.BlockSpec((tm, D_MODEL), lambda i: (i, 0)),
            pl.BlockSpec((1, D_MODEL), const),
            pl.BlockSpec((D_MODEL, LANES), const),
            pl.BlockSpec((1, LANES), const),
            pl.BlockSpec((D_MODEL, N_NAT), const, **resident),
            pl.BlockSpec((nfeat, D_MODEL), const, **resident),
            pl.BlockSpec((1, LANES), const),
            pl.BlockSpec((ROPE_HALF, 1), const),
            pl.BlockSpec((M_CONV, 2 * N_MQK), const),
            pl.BlockSpec((1, 2 * N_MQK), const),
        ],
        out_specs=[
            pl.BlockSpec((tm, 2 * M_HEADS), lambda i: (i, 0)),
            pl.BlockSpec((tm // LANES, 2 * M_HEADS, LANES), lambda i: (i, 0, 0)),
            pl.BlockSpec((tm, N_NAT), lambda i: (i, 0)),
            pl.BlockSpec((1, nfeat, ATT_BLK), lambda i: (i, 0, 0)),
        ],
        out_shape=[
            jax.ShapeDtypeStruct((S, 2 * M_HEADS), F32),
            jax.ShapeDtypeStruct((S // LANES, 2 * M_HEADS, LANES), F32),
            jax.ShapeDtypeStruct((S, N_NAT), BF16),
            jax.ShapeDtypeStruct((S // ATT_BLK, nfeat, ATT_BLK), BF16),
        ],
        scratch_shapes=[
            pltpu.VMEM((tm, D_MODEL), BF16),
            pltpu.VMEM((tm + 8, FRONT_TN), F32),
            pltpu.VMEM((2 * N_MQK // FRONT_TN, 8, FRONT_TN), F32),
        ],
        compiler_params=_cparams(("arbitrary",)),
        name="front",
    )(x2d, g, wg, bg, w_nat, w_t, inv_row, inv_col, conv_w, conv_b)


def _split3(x):
    hi = x.astype(BF16)
    r1 = x - hi.astype(F32)
    mid = r1.astype(BF16)
    lo = (r1 - mid.astype(F32)).astype(BF16)
    return hi, mid, lo


def _mlstm_kernel(pm_ref, gc_ref, gr_ref, mg_ref, o_ref, c_ref, m_ref):
    R = pm_ref.shape[0]
    L = ML_L
    n_chunks = R // L
    NQK = 2 * N_MQK

    @pl.when(pl.program_id(0) == 0)
    def _():
        c_ref[...] = jnp.zeros_like(c_ref)
        m_ref[...] = jnp.zeros_like(m_ref)

    t_idx = lax.broadcasted_iota(jnp.int32, (L, L), 0)
    s_idx = lax.broadcasted_iota(jnp.int32, (L, L), 1)
    causal = s_idx <= t_idx
    tril = causal.astype(BF16)
    triu = (t_idx <= s_idx).astype(BF16)
    ones_ext = jnp.ones((L, ONES_W), BF16)

    H = range(M_HEADS)

    def rows_of(c):
        return slice(c * L, (c + 1) * L)

    for c0 in range(0, n_chunks, ML_GROUP):
        cs = range(c0, min(c0 + ML_GROUP, n_chunks))
        G = [(c, h) for c in cs for h in H]
        gr = {c: gr_ref[c] for c in cs}
        b_col_all = {c: sum(jnp.dot(tril, p, preferred_element_type=F32)
                            for p in _split3(gc_ref[rows_of(c), :])) for c in cs}
        b_row_all = {c: sum(jnp.dot(p, triu, preferred_element_type=F32)
                            for p in _split3(gr[c])) for c in cs}
        qh = {(c, h): pm_ref[rows_of(c), h * M_DQK:(h + 1) * M_DQK] for c, h in G}
        kh = {(c, h): pm_ref[rows_of(c), N_MQK + h * M_DQK:N_MQK + (h + 1) * M_DQK] for c, h in G}
        v_ext = {(c, h): jnp.concatenate(
            [pm_ref[rows_of(c), NQK + h * M_DV:NQK + (h + 1) * M_DV], ones_ext], axis=1) for c, h in G}
        i_row = {(c, h): gr[c][h:h + 1, :] for c, h in G}
        b_row = {(c, h): b_row_all[c][M_HEADS + h:M_HEADS + h + 1, :] for c, h in G}
        b_col = {(c, h): b_col_all[c][:, M_HEADS + h:M_HEADS + h + 1] for c, h in G}

        logd = {g: jnp.where(causal, b_col[g] - b_row[g] + i_row[g], -jnp.inf) for g in G}
        m_loc = {g: jnp.max(logd[g], axis=1, keepdims=True) for g in G}
        d_loc = {g: jnp.exp(logd[g] - m_loc[g]) for g in G}
        s = {g: lax.dot_general(qh[g], kh[g], (((1,), (1,)), ((), ())), preferred_element_type=F32) for g in G}
        sd = {g: (s[g] * d_loc[g]).astype(BF16) for g in G}
        p_ext = {g: jnp.dot(sd[g], v_ext[g], preferred_element_type=F32) for g in G}

        b_last = {g: b_row[g][:, L - 1:L] for g in G}
        w_log = {g: b_last[g] - b_row[g] + i_row[g] for g in G}
        a_loc = {g: jnp.max(w_log[g], axis=1, keepdims=True) for g in G}
        kw_t = {g: (kh[g].astype(F32).T * jnp.exp(w_log[g] - a_loc[g])).astype(BF16) for g in G}
        kv_ext = {g: jnp.dot(kw_t[g], v_ext[g], preferred_element_type=F32) for g in G}

        for c in cs:
            rows = rows_of(c)
            m_prev = [m_ref[h:h + 1, 0:1] for h in H]
            c_prev = [c_ref[h] for h in H]
            inter = [b_col[c, h] + m_prev[h] for h in H]
            m_t = [jnp.maximum(inter[h], m_loc[c, h]) for h in H]
            qc = [jnp.dot(qh[c, h], c_prev[h].astype(BF16), preferred_element_type=F32) for h in H]
            nd = [jnp.exp(inter[h] - m_t[h]) * qc[h] + jnp.exp(m_loc[c, h] - m_t[h]) * p_ext[c, h] for h in H]
            hh = [nd[h][:, :M_DV] / jnp.maximum(jnp.abs(nd[h][:, M_DV:M_DV + 1]), jnp.exp(-m_t[h])) for h in H]

            m_new = [jnp.maximum(b_last[c, h] + m_prev[h], a_loc[c, h]) for h in H]
            for h in H:
                c_ref[h] = (jnp.exp(b_last[c, h] + m_prev[h] - m_new[h]) * c_prev[h]
                            + jnp.exp(a_loc[c, h] - m_new[h]) * kv_ext[c, h])
                m_ref[h:h + 1, :] = jnp.broadcast_to(m_new[h], (1, m_ref.shape[1]))

            ms = [jnp.mean(hh[h] * hh[h], axis=-1, keepdims=True) for h in H]
            for h in H:
                cols = slice(h * M_DV, (h + 1) * M_DV)
                hn = (hh[h] * lax.rsqrt(ms[h] + EPS)) * mg_ref[:, cols]
                mo = pm_ref[rows, NQK + M_WIDTH + h * M_DV:NQK + M_WIDTH + (h + 1) * M_DV].astype(F32)
                o_ref[rows, cols] = (hn * jax.nn.sigmoid(mo)).astype(o_ref.dtype)


def _mlstm(pn, gc, gr, mnorm_g):
    S = pn.shape[0]
    R = ML_R
    wm = 2 * N_MQK + 2 * M_WIDTH
    return pl.pallas_call(
        _mlstm_kernel,
        grid=(S // R,),
        in_specs=[
            pl.BlockSpec((R, wm), lambda i: (i, 0)),
            pl.BlockSpec((R, 2 * M_HEADS), lambda i: (i, 0)),
            pl.BlockSpec((R // ML_L, 2 * M_HEADS, ML_L), lambda i: (i, 0, 0)),
            pl.BlockSpec((1, M_WIDTH), lambda i: (0, 0)),
        ],
        out_specs=pl.BlockSpec((R, M_WIDTH), lambda i: (i, 0)),
        out_shape=jax.ShapeDtypeStruct((S, M_WIDTH), BF16),
        scratch_shapes=[
            pltpu.VMEM((M_HEADS, M_DQK, M_DV + ONES_W), F32),
            pltpu.VMEM((8, LANES), F32),
        ],
        compiler_params=_cparams(("arbitrary",)),
        name="mlstm",
    )(pn, gc, gr, mnorm_g)


def _attn_kernel(qT_ref, k_ref, vT_ref, lam_ref, g_ref, o_ref,
                 qz_ref, p_ref, m_ref, l_ref, acc_ref, *, lam_init):
    qi = pl.program_id(1)
    tq, tk, blk = ATT_TQ, ATT_TK, ATT_BLK
    n_blocks = (qi + 1) * (tq // tk)
    ncol = 2 * tq
    wa = 256

    frow = lax.broadcasted_iota(jnp.int32, (2 * A_DH, blk), 0)
    for b in range(tq // blk):
        qb = qT_ref[b]
        zero = jnp.zeros_like(qb)
        qz_ref[:, b * blk:(b + 1) * blk] = jnp.where(frow < A_DH, qb, zero)
        qz_ref[:, tq + b * blk:tq + (b + 1) * blk] = jnp.where(frow >= A_DH, qb, zero)

    def key_block(t):
        return k_ref[pl.ds(pl.multiple_of(t * tk, tk), tk), :]

    def diag_visible(c, d):
        kc = (lax.broadcasted_iota(jnp.int32, (tk, wa), 0) + d * tk) // CHUNK
        qc = ((lax.broadcasted_iota(jnp.int32, (tk, wa), 1) + c * wa) % tq) // CHUNK
        return kc <= qc

    def scores(kb, c):
        return jnp.dot(kb, qz_ref[:, c * wa:(c + 1) * wa], preferred_element_type=F32)

    all_chunks = tuple(range(ncol // wa))
    n_diag = tq // tk
    first_diag = qi * n_diag

    def chunks_from(d):
        return tuple(c for c in all_chunks if (c * wa) % tq >= d * tk)

    k_first = k_ref[0:CHUNK, :]
    for c in all_chunks:
        cols = slice(c * wa, (c + 1) * wa)
        m_ref[:, cols] = jnp.max(scores(k_first, c), axis=0, keepdims=True)
    l_ref[...] = jnp.zeros_like(l_ref)
    acc_ref[...] = jnp.zeros_like(acc_ref)

    def stage_ab(t, par, diag=None, chunks=all_chunks):
        kb = key_block(t)
        for c in chunks:
            cols = slice(c * wa, (c + 1) * wa)
            s = scores(kb, c)
            if diag is not None:
                s = jnp.where(diag_visible(c, diag), s, -jnp.inf)
            p = jnp.exp2(s - m_ref[:, cols])
            l_ref[:, cols] += jnp.sum(p, axis=0, keepdims=True)
            p_ref[par, :, cols] = p.astype(BF16)

    def stage_c(t, par, chunks=all_chunks):
        vb = vT_ref[t]
        for c in chunks:
            cols = slice(c * wa, (c + 1) * wa)
            acc_ref[:, cols] += jnp.dot(vb, p_ref[par, :, cols], preferred_element_type=F32)

    order = list(range(n_diag - 1, -1, -1))
    stage_ab(first_diag + order[0], 0, diag=order[0], chunks=chunks_from(order[0]))
    for k in range(1, n_diag):
        stage_c(first_diag + order[k - 1], (k - 1) % 2, chunks=chunks_from(order[k - 1]))
        stage_ab(first_diag + order[k], k % 2, diag=order[k], chunks=chunks_from(order[k]))

    @pl.when(qi == 0)
    def _():
        stage_c(first_diag, 1)

    @pl.when(qi > 0)
    def _():
        stage_c(first_diag, 1)
        stage_ab(0, 0)

        def pair(t):
            stage_c(t, 0)
            stage_ab(t + 1, 1)
            stage_c(t + 1, 1)
            stage_ab(t + 2, 0)

        def two_pairs(v, carry):
            pair(4 * v)
            pair(4 * v + 2)
            return carry

        n_pairs = lax.shift_right_logical(first_diag, 1) - 1
        lax.fori_loop(0, lax.shift_right_logical(n_pairs, 1), two_pairs, 0)

        @pl.when(jnp.bitwise_and(n_pairs, 1) == 1)
        def _():
            pair(2 * (n_pairs - 1))

        stage_c(first_diag - 2, 0)
        stage_ab(first_diag - 1, 1)
        stage_c(first_diag - 1, 1)

    l_fast = l_ref[...]
    overflow = jnp.max(jnp.where(l_fast < ATT_L_LIMIT, 0.0, 1.0)) > 0.5

    @pl.when(overflow)
    def _():
        m_ref[...] = jnp.full_like(m_ref, -jnp.inf)
        l_ref[...] = jnp.zeros_like(l_ref)
        acc_ref[...] = jnp.zeros_like(acc_ref)

        def block(t, carry):
            kb = key_block(t)
            vb = vT_ref[t]
            for c in range(ncol // wa):
                cols = slice(c * wa, (c + 1) * wa)
                kc = (lax.broadcasted_iota(jnp.int32, (tk, wa), 0) + t * tk) // CHUNK
                qc = ((lax.broadcasted_iota(jnp.int32, (tk, wa), 1) + c * wa) % tq + qi * tq) // CHUNK
                s = jnp.where(kc <= qc, scores(kb, c), -jnp.inf)
                m_old = m_ref[:, cols]
                m_new = jnp.maximum(m_old, jnp.max(s, axis=0, keepdims=True))
                alpha = jnp.exp2(m_old - m_new)
                p = jnp.exp2(s - m_new)
                l_ref[:, cols] = alpha * l_ref[:, cols] + jnp.sum(p, axis=0, keepdims=True)
                m_ref[:, cols] = m_new
                acc_ref[:, cols] = alpha * acc_ref[:, cols] + jnp.dot(
                    vb, p.astype(BF16), preferred_element_type=F32)
            return carry

        lax.fori_loop(0, n_blocks, block, 0)

    lam_p = lam_ref[...]
    lam = (jnp.exp(jnp.sum(lam_p[0:1] * lam_p[1:2], axis=1, keepdims=True))
           - jnp.exp(jnp.sum(lam_p[2:3] * lam_p[3:4], axis=1, keepdims=True)) + lam_init)
    acc = acc_ref[...]
    l = l_ref[...]
    out = acc[:, :tq] / l[:, :tq] - lam * (acc[:, tq:] / l[:, tq:])
    ms = jnp.mean(out * out, axis=0, keepdims=True)
    y = ((out * lax.rsqrt(ms + EPS)) * g_ref[...]) * (1.0 - lam_init)
    o_ref[...] = y.T.astype(o_ref.dtype)


def _attn(qvT, pn, lam_params, subln_col, lam_init):
    nblk = qvT.shape[0]
    S = pn.shape[0]
    tq, tk, blk = ATT_TQ, ATT_TK, ATT_BLK
    assert tk == blk and tq % (2 * tk) == 0
    k_col0 = AK_OFF // (2 * A_DH)
    return pl.pallas_call(
        functools.partial(_attn_kernel, lam_init=lam_init),
        grid=(A_HEADS, S // tq),
        in_specs=[
            pl.BlockSpec((tq // blk, 2 * A_DH, blk), lambda h, i: (i, h, 0)),
            pl.BlockSpec((S, 2 * A_DH), lambda h, i: (0, k_col0 + h)),
            pl.BlockSpec((nblk, A_DV, blk), lambda h, i: (0, A_HEADS + h, 0)),
            pl.BlockSpec((4, A_DH), lambda h, i: (0, 0)),
            pl.BlockSpec((A_DV, 1), lambda h, i: (0, 0)),
        ],
        out_specs=pl.BlockSpec((tq, A_DV), lambda h, i: (i, h)),
        out_shape=jax.ShapeDtypeStruct((S, A_WIDTH), BF16),
        scratch_shapes=[
            pltpu.VMEM((2 * A_DH, 2 * tq), BF16),
            pltpu.VMEM((2, tk, 2 * tq), BF16),
            pltpu.VMEM((1, 2 * tq), F32),
            pltpu.VMEM((1, 2 * tq), F32),
            pltpu.VMEM((A_DV, 2 * tq), F32),
        ],
        compiler_params=_cparams(("parallel", "arbitrary")),
        name="diff_attn",
    )(qvT, pn, qvT, lam_params, subln_col)


def _out_proj_kernel(hm_ref, ha_ref, w_ref, x_ref, g_ref, x1_ref, h2_ref):
    r = jnp.dot(hm_ref[...], w_ref[0:M_WIDTH, :], preferred_element_type=F32)
    r = r + jnp.dot(ha_ref[...], w_ref[M_WIDTH:, :], preferred_element_type=F32)
    x1 = x_ref[...] + r
    x1_ref[...] = x1
    ms = jnp.mean(x1 * x1, axis=-1, keepdims=True)
    h2_ref[...] = ((x1 * lax.rsqrt(ms + EPS)) * g_ref[...]).astype(h2_ref.dtype)


def _out_proj(hm, ha, w_out, x2d, g):
    S = x2d.shape[0]
    tm = OUT_TM
    return pl.pallas_call(
        _out_proj_kernel,
        grid=(S // tm,),
        in_specs=[
            pl.BlockSpec((tm, M_WIDTH), lambda i: (i, 0)),
            pl.BlockSpec((tm, A_WIDTH), lambda i: (i, 0)),
            pl.BlockSpec((M_WIDTH + A_WIDTH, D_MODEL), lambda i: (0, 0)),
            pl.BlockSpec((tm, D_MODEL), lambda i: (i, 0)),
            pl.BlockSpec((1, D_MODEL), lambda i: (0, 0)),
        ],
        out_specs=[
            pl.BlockSpec((tm, D_MODEL), lambda i: (i, 0)),
            pl.BlockSpec((tm, D_MODEL), lambda i: (i, 0)),
        ],
        out_shape=[
            jax.ShapeDtypeStruct((S, D_MODEL), F32),
            jax.ShapeDtypeStruct((S, D_MODEL), BF16),
        ],
        compiler_params=_cparams(("parallel",)),
        name="out_proj",
    )(hm, ha, w_out, x2d, g)


def _swiglu_kernel(h2_ref, wg_ref, wu_ref, wd_ref, x1_ref, fg_ref, o_ref, acc_ref):
    j = pl.program_id(1)

    @pl.when(j == 0)
    def _():
        acc_ref[...] = jnp.zeros_like(acc_ref)

    h2 = h2_ref[...]
    g = jnp.dot(h2, wg_ref[...], preferred_element_type=F32)
    u = jnp.dot(h2, wu_ref[...], preferred_element_type=F32)
    a = ((g * jax.nn.sigmoid(g)) * u).astype(BF16)
    acc_ref[...] += jnp.dot(a, wd_ref[...], preferred_element_type=F32)

    @pl.when(j == pl.num_programs(1) - 1)
    def _():
        y = x1_ref[...] + acc_ref[...]
        ms = jnp.mean(y * y, axis=-1, keepdims=True)
        o_ref[...] = (y * lax.rsqrt(ms + EPS)) * fg_ref[...]


def _swiglu(h2, wg, wu, wd, x1, fg):
    S = h2.shape[0]
    tm, tf = FF_TM, FF_TF
    return pl.pallas_call(
        _swiglu_kernel,
        grid=(S // tm, D_FF // tf),
        in_specs=[
            pl.BlockSpec((tm, D_MODEL), lambda i, j: (i, 0)),
            pl.BlockSpec((D_MODEL, tf), lambda i, j: (0, j)),
            pl.BlockSpec((D_MODEL, tf), lambda i, j: (0, j)),
            pl.BlockSpec((tf, D_MODEL), lambda i, j: (j, 0)),
            pl.BlockSpec((tm, D_MODEL), lambda i, j: (i, 0)),
            pl.BlockSpec((1, D_MODEL), lambda i, j: (0, 0)),
        ],
        out_specs=pl.BlockSpec((tm, D_MODEL), lambda i, j: (i, 0)),
        out_shape=jax.ShapeDtypeStruct((S, D_MODEL), F32),
        scratch_shapes=[pltpu.VMEM((tm, D_MODEL), F32)],
        compiler_params=_cparams(("parallel", "arbitrary")),
        name="swiglu",
    )(h2, wg, wu, wd, x1, fg)


O_MI = 2 * N_MQK + 2 * M_WIDTH
O_AQ = O_MI + 2 * M_HEADS
O_AK = O_AQ + A_WIDTH
O_AV = O_AK + A_WIDTH
N_IN = O_AV + A_WIDTH


def _wprep_kernel(w_ref, wn_ref, wt_ref, wg_ref):
    tr = w_ref.shape[0]
    wn_ref[:, 0:O_MI] = w_ref[:, 0:O_MI].astype(BF16)
    wn_ref[:, O_MI:N_NAT] = w_ref[:, O_AK:O_AV].astype(BF16)
    wg_ref[...] = jnp.concatenate(
        [w_ref[:, O_MI:O_AQ], jnp.zeros((tr, LANES - 2 * M_HEADS), F32)], axis=1).astype(BF16)
    wt_ref[0:A_WIDTH, :] = w_ref[:, O_AQ:O_AK].T.astype(BF16)
    wt_ref[A_WIDTH:2 * A_WIDTH, :] = w_ref[:, O_AV:N_IN].T.astype(BF16)


def _wprep(w):
    tr = WPREP_TR
    return pl.pallas_call(
        _wprep_kernel,
        grid=(D_MODEL // tr,),
        in_specs=[pl.BlockSpec((tr, N_IN), lambda i: (i, 0))],
        out_specs=[
            pl.BlockSpec((tr, N_NAT), lambda i: (i, 0)),
            pl.BlockSpec((2 * A_WIDTH, tr), lambda i: (0, i)),
            pl.BlockSpec((tr, LANES), lambda i: (i, 0)),
        ],
        out_shape=[
            jax.ShapeDtypeStruct((D_MODEL, N_NAT), BF16),
            jax.ShapeDtypeStruct((2 * A_WIDTH, D_MODEL), BF16),
            jax.ShapeDtypeStruct((D_MODEL, LANES), BF16),
        ],
        compiler_params=_cparams(("parallel",)),
        name="wprep",
    )(w)


def _rope_inv_freq():
    inv = ROPE_THETA ** (-jnp.arange(ROPE_HALF, dtype=F32) / ROPE_HALF)
    return jnp.tile(inv, LANES // ROPE_HALF).reshape(1, LANES), inv.reshape(ROPE_HALF, 1)


def kernel(x, norm1_g, w_in, conv_w, conv_b, b_igate, b_fgate, mnorm_g,
           lambda_q1, lambda_k1, lambda_q2, lambda_k2, subln_g, w_out,
           norm2_g, w_gate, w_up, w_down, final_g):
    B, S, D = x.shape
    assert (B, S, D) == (1, SEQ, D_MODEL) and norm1_g.shape[0] == 1
    l = 0
    x2d = x.reshape(S, D)

    assert w_in.shape[1:] == (D_MODEL, N_IN)
    w_nat, w_t, w_g = _wprep(w_in[l])
    b_g =jnp.pad(jnp.concatenate([b_igate[l], b_fgate[l]]), (0, LANES - 2 * M_HEADS)).reshape(1, LANES)

    inv_row, inv_col = _rope_inv_freq()

    gc, gr, pn, qvT = _front(x2d, norm1_g[l].reshape(1, D), w_g, b_g, w_nat, w_t,
                             inv_row, inv_col, conv_w[l], conv_b[l].reshape(1, -1))

    hm = _mlstm(pn, gc, gr, mnorm_g[l].reshape(1, M_WIDTH))

    lam_init = 0.8 - 0.6 * math.exp(-0.3 * l)
    lam_params = jnp.stack([lambda_q1[l], lambda_k1[l], lambda_q2[l], lambda_k2[l]]).astype(F32)
    ha = _attn(qvT, pn, lam_params, subln_g[l].reshape(A_DV, 1), lam_init)

    x1, h2 = _out_proj(hm, ha, w_out[l].astype(BF16), x2d, norm2_g[l].reshape(1, D))
    out = _swiglu(h2, w_gate[l].astype(BF16), w_up[l].astype(BF16), w_down[l].astype(BF16),
                  x1, final_g.reshape(1, D))
    return out.reshape(B, S, D)
```
